```python
import math
import jax, jax.numpy as jnp
from jax import lax
import numpy as np

D_MODEL = 1024
BATCH = 2
SEQ = 8192
DEPTH = 1
DEC_BATCH = 32
DEC_SEQ = 1
PAST_LEN = 8192
PAGE_SIZE = 128

HEAD_DIM = 64
N_HEADS = D_MODEL // HEAD_DIM
H_DN = N_HEADS // 2
H_ATT = N_HEADS - H_DN
D_DN = H_DN * HEAD_DIM
D_ATT = H_ATT * HEAD_DIM
D_CONV = 3 * D_DN
CONV_W = 4
DN_CHUNK = 64
MOBA_BLOCK = 256
MOBA_TOPK = 3
Q_CHUNK = 64
N_BUCKETS = 32
MAX_DISTANCE = 128
D_FF = 4 * D_MODEL
D_PLE = 256
EPS = 1e-6
SPLITS = [D_CONV, D_CONV + D_DN, D_CONV + D_DN + H_DN, D_CONV + D_DN + 2 * H_DN,
          D_CONV + D_DN + 2 * H_DN + D_ATT, D_CONV + D_DN + 2 * H_DN + 2 * D_ATT]
D_IN = D_CONV + D_DN + 2 * H_DN + 3 * D_ATT

kernel_name = "hymba_gdn_moba_decode_step"

F32 = jnp.float32


def rmsnorm(x, g):
    xf = x.astype(F32)
    y = xf * lax.rsqrt(jnp.mean(xf * xf, -1, keepdims=True) + EPS)
    return (y * g.astype(F32)).astype(x.dtype)


def l2norm(x):
    xf = x.astype(F32)
    return xf * lax.rsqrt(jnp.sum(xf * xf, -1, keepdims=True) + EPS)


def causal_conv(x, buf, w):
    T = x.shape[1]
    xp = jnp.concatenate([buf.astype(x.dtype), x], axis=1)
    y = sum(xp[:, i:i + T] * w[i] for i in range(CONV_W))
    return jax.nn.silu(y), xp[:, -(CONV_W - 1):]


def project(xn, w_in):
    z = xn @ w_in
    return jnp.split(z, SPLITS, axis=-1)


def dn_prepare(qkv_raw, a, b, conv_buf, conv_w, a_log, dt_bias):
    B, T, _ = qkv_raw.shape
    qkv, new_buf = causal_conv(qkv_raw, conv_buf, conv_w)
    qkv = qkv.reshape(B, T, 3, H_DN, HEAD_DIM)
    q = l2norm(qkv[:, :, 0]) * HEAD_DIM ** -0.5
    k = l2norm(qkv[:, :, 1])
    v = qkv[:, :, 2].astype(F32)
    g = -jnp.exp(a_log.astype(F32)) * jax.nn.softplus(a.astype(F32) + dt_bias.astype(F32))
    beta = jax.nn.sigmoid(b.astype(F32))
    return q, k, v, g, beta, new_buf


def gated_delta_chunked(q, k, v, g, beta):
    B, T, H, d = q.shape
    C = DN_CHUNK
    n = T // C

    def chunks(x):
        return jnp.moveaxis(x.reshape((B, n, C, H) + x.shape[3:]), 3, 1)

    qc, kc, vc, gc, bc = chunks(q), chunks(k), chunks(v), chunks(g), chunks(beta)
    gam = jnp.cumsum(gc, axis=-1)
    causal = jnp.tril(jnp.ones((C, C), bool))
    strict = jnp.tril(jnp.ones((C, C), bool), -1)
    diff = gam[..., :, None] - gam[..., None, :]
    decay_mat = jnp.where(causal, jnp.exp(jnp.where(causal, diff, 0.0)), 0.0)
    kk = jnp.einsum('bhncd,bhned->bhnce', kc, kc)
    A = jnp.where(strict, bc[..., :, None] * kk * decay_mat, 0.0)
    eye = jnp.eye(C, dtype=F32)
    Tinv = lax.linalg.triangular_solve(eye + A, jnp.broadcast_to(eye, A.shape),
                                       left_side=True, lower=True)
    u = jnp.einsum('bhnij,bhnjd->bhnid', Tinv, vc * bc[..., None])
    w = jnp.einsum('bhnij,bhnjd->bhnid', Tinv, kc * (bc * jnp.exp(gam))[..., None])
    qk = jnp.where(causal, jnp.einsum('bhncd,bhned->bhnce', qc, kc) * decay_mat, 0.0)
    q_dec = qc * jnp.exp(gam)[..., None]
    k_dec = kc * jnp.exp(gam[..., -1:] - gam)[..., None]
    g_last = jnp.exp(gam[..., -1])

    def step(S, xs):
        u_i, w_i, qk_i, qd_i, kd_i, gl_i = xs
        v_new = u_i - jnp.einsum('bhcd,bhde->bhce', w_i, S)
        o_i = jnp.einsum('bhcd,bhde->bhce', qd_i, S) + jnp.einsum('bhij,bhje->bhie', qk_i, v_new)
        S = S * gl_i[..., None, None] + jnp.einsum('bhcd,bhce->bhde', kd_i, v_new)
        return S, o_i

    xs = tuple(jnp.moveaxis(t, 2, 0) for t in (u, w, qk, q_dec, k_dec, g_last))
    S_fin, o = lax.scan(step, jnp.zeros((B, H, d, d), F32), xs)
    o = jnp.moveaxis(o, 0, 2).reshape(B, H, T, d)
    return jnp.transpose(o, (0, 2, 1, 3)), S_fin


def gated_delta_recurrent(q, k, v, g, beta, S0):
    def step(S, xs):
        q_t, k_t, v_t, g_t, b_t = xs
        S = S * jnp.exp(g_t)[..., None, None]
        err = v_t - jnp.einsum('bhde,bhd->bhe', S, k_t)
        S = S + jnp.einsum('bhd,bhe->bhde', k_t * b_t[..., None], err)
        return S, jnp.einsum('bhde,bhd->bhe', S, q_t)

    xs = tuple(jnp.moveaxis(t, 1, 0) for t in (q, k, v, g, beta))
    S, o = lax.scan(step, S0.astype(F32), xs)
    return jnp.moveaxis(o, 0, 1), S


def dn_output(o, gate, g_dn, dtype):
    B, T = gate.shape[:2]
    y = rmsnorm(o, g_dn) * jax.nn.silu(gate.astype(F32).reshape(B, T, H_DN, HEAD_DIM))
    return y.reshape(B, T, D_DN).astype(dtype)


def t5_bucket(n):
    max_exact = N_BUCKETS // 2
    nf = jnp.maximum(n, 1).astype(F32)
    large = max_exact + (jnp.log(nf / max_exact) / math.log(MAX_DISTANCE / max_exact)
                         * (N_BUCKETS - max_exact)).astype(jnp.int32)
    large = jnp.minimum(large, N_BUCKETS - 1)
    return jnp.where(n < max_exact, n, large)


def block_means(k):
    B, T, H, d = k.shape
    nb = -(-T // MOBA_BLOCK)
    kp = jnp.pad(k.astype(F32), ((0, 0), (0, nb * MOBA_BLOCK - T), (0, 0), (0, 0)))
    return kp.reshape(B, nb, MOBA_BLOCK, H, d).mean(axis=2)


def _bh_index(B, H, extra):
    bi = jnp.arange(B).reshape((B, 1) + (1,) * extra)
    hi = jnp.arange(H).reshape((1, H) + (1,) * extra)
    return bi, hi


def rows_fetcher(x):
    B, T, H, _ = x.shape

    def fetch(pos):
        bi, hi = _bh_index(B, H, pos.ndim - 2)
        return x[bi, jnp.clip(pos, 0, T - 1), hi]
    return fetch


def paged_fetcher(cache, page_table, new_rows):
    B, Tn, H, _ = new_rows.shape
    n_pages = page_table.shape[1]
    past_len = n_pages * PAGE_SIZE

    def fetch(pos):
        bi, hi = _bh_index(B, H, pos.ndim - 2)
        phys = page_table[bi, jnp.clip(pos // PAGE_SIZE, 0, n_pages - 1)]
        from_cache = cache[phys, pos % PAGE_SIZE, hi]
        from_new = new_rows[bi, jnp.clip(pos - past_len, 0, Tn - 1), hi]
        return jnp.where((pos < past_len)[..., None], from_cache.astype(F32), from_new.astype(F32))
    return fetch


def moba_attend(q, qpos, k_means, fetch_k, fetch_v, rel_bias):
    B, Q, H, d = q.shape
    nb = k_means.shape[1]
    qf = q.astype(F32)
    own = qpos // MOBA_BLOCK
    s = jnp.einsum('bqhd,bnhd->bhqn', qf, k_means)
    s = jnp.where(jnp.arange(nb)[None, :] < own[:, None], s, -jnp.inf)
    if nb < MOBA_TOPK:
        s = jnp.pad(s, ((0, 0), (0, 0), (0, 0), (0, MOBA_TOPK - nb)), constant_values=-jnp.inf)
    _, idx = lax.top_k(s, MOBA_TOPK)
    own_b = jnp.broadcast_to(own[None, None, :, None], (B, H, Q, 1))
    blk = jnp.concatenate([idx, own_b], axis=-1)
    ok = jnp.concatenate([idx < own_b, jnp.ones((B, H, Q, 1), bool)], axis=-1)
    kpos = blk[..., None] * MOBA_BLOCK + jnp.arange(MOBA_BLOCK)
    qp = qpos[None, None, :, None, None]
    mask = ok[..., None] & (kpos <= qp)
    ks = fetch_k(kpos).astype(F32)
    vs = fetch_v(kpos).astype(F32)
    _, hi = _bh_index(B, H, 3)
    bias = rel_bias.astype(F32)[t5_bucket(jnp.maximum(qp - kpos, 0)), hi]
    logits = jnp.einsum('bqhd,bhqjkd->bhqjk', qf, ks) * HEAD_DIM ** -0.5 + bias
    logits = jnp.where(mask, logits, -jnp.inf).reshape(B, H, Q, -1)
    p = jax.nn.softmax(logits, axis=-1).reshape(mask.shape)
    return jnp.einsum('bhqjk,bhqjkd->bqhd', p, vs)


def moba_prompt(q, k, v, rel_bias):
    B, T, H, d = q.shape
    km = block_means(k)
    fk, fv = rows_fetcher(k), rows_fetcher(v)

    def one(c):
        qc = lax.dynamic_slice_in_dim(q, c * Q_CHUNK, Q_CHUNK, axis=1)
        qpos = c * Q_CHUNK + jnp.arange(Q_CHUNK, dtype=jnp.int32)
        return moba_attend(qc, qpos, km, fk, fv, rel_bias)

    o = lax.map(one, jnp.arange(T // Q_CHUNK, dtype=jnp.int32))
    return jnp.moveaxis(o, 0, 1).reshape(B, T, H, d)


def mixer_prompt(xn, w_in, conv_w, a_log, dt_bias, g_dn, rel_bias):
    B, T, _ = xn.shape
    qkv_raw, gate, a, b, qa, ka, va = project(xn, w_in)
    buf0 = jnp.zeros((B, CONV_W - 1, D_CONV), xn.dtype)
    q, k, v, g, beta, conv_new = dn_prepare(qkv_raw, a, b, buf0, conv_w, a_log, dt_bias)
    o_dn, s_new = gated_delta_chunked(q, k, v, g, beta)
    qa, ka, va = (t.reshape(B, T, H_ATT, HEAD_DIM) for t in (qa, ka, va))
    o_at = moba_prompt(qa, ka, va, rel_bias)
    out = jnp.concatenate([dn_output(o_dn, gate, g_dn, xn.dtype),
                           o_at.reshape(B, T, D_ATT).astype(xn.dtype)], axis=-1)
    return out, ka, va, s_new, conv_new


def mixer_sample(xn, cache_k, cache_v, s_dn, conv_buf, page_table, w_in, conv_w, a_log, dt_bias, g_dn, rel_bias):
    B, T, _ = xn.shape
    qkv_raw, gate, a, b, qa, ka, va = project(xn, w_in)
    q, k, v, g, beta, conv_new = dn_prepare(qkv_raw, a, b, conv_buf, conv_w, a_log, dt_bias)
    o_dn, s_new = gated_delta_recurrent(q, k, v, g, beta, s_dn)
    qa, ka, va = (t.reshape(B, T, H_ATT, HEAD_DIM) for t in (qa, ka, va))
    past_len = page_table.shape[1] * PAGE_SIZE
    k_past = cache_k[page_table].reshape(B, past_len, H_ATT, HEAD_DIM)
    k_all = jnp.concatenate([k_past, ka.astype(k_past.dtype)], axis=1)
    km = block_means(k_all)
    qpos = past_len + jnp.arange(T, dtype=jnp.int32)
    o_at = moba_attend(qa, qpos, km, rows_fetcher(k_all), paged_fetcher(cache_v, page_table, va), rel_bias)
    out = jnp.concatenate([dn_output(o_dn, gate, g_dn, xn.dtype),
                           o_at.reshape(B, T, D_ATT).astype(xn.dtype)], axis=-1)
    return out, ka, va, s_new, conv_new


def block_tail(h, mix, p_i, w_o, g_post_mix, g_pre_mlp, w_up, w_down, g_post_mlp, g_ple, w_pg, w_ple):
    h = h + rmsnorm(mix @ w_o, g_post_mix)
    u = jax.nn.relu(rmsnorm(h, g_pre_mlp) @ w_up)
    h = h + rmsnorm((u * u) @ w_down, g_post_mlp)
    gate = jax.nn.sigmoid(rmsnorm(h, g_ple) @ w_pg)
    return h + (p_i @ w_ple) * gate


def setup_inputs(seed: int = 0) -> dict:
    key = jax.random.key(seed)
    ks = jax.random.split(key, 32)

    def nrm(k, shape, scale):
        return jax.random.normal(k, shape, F32) * scale

    def gain(k, shape):
        return 1.0 + 0.05 * jax.random.normal(k, shape, F32)

    n_pages = PAST_LEN // PAGE_SIZE
    n_pool = (DEC_BATCH * n_pages * 5) // 4
    perm = jax.random.permutation(ks[0], n_pool)
    page_table = perm[:DEC_BATCH * n_pages].reshape(DEC_BATCH, n_pages).astype(jnp.int32)
    dt = jnp.exp(jax.random.uniform(ks[1], (DEPTH, H_DN), F32, math.log(1e-3), math.log(1e-1)))
    return {
        "x_prompt": nrm(ks[2], (BATCH, SEQ, D_MODEL), 1.0),
        "x_sample": nrm(ks[3], (DEC_BATCH, DEC_SEQ, D_MODEL), 1.0),
        "p_prompt": nrm(ks[4], (DEPTH, BATCH, SEQ, D_PLE), 1.0),
        "p_sample": nrm(ks[5], (DEPTH, DEC_BATCH, DEC_SEQ, D_PLE), 1.0),
        "cache_k": nrm(ks[6], (DEPTH, n_pool, PAGE_SIZE, H_ATT, HEAD_DIM), 1.0),
        "cache_v": nrm(ks[7], (DEPTH, n_pool, PAGE_SIZE, H_ATT, HEAD_DIM), 1.0),
        "state_dn": nrm(ks[8], (DEPTH, DEC_BATCH, H_DN, HEAD_DIM, HEAD_DIM), 0.1),
        "state_conv": nrm(ks[9], (DEPTH, DEC_BATCH, CONV_W - 1, D_CONV), 1.0),
        "page_table": page_table,
        "w_in": nrm(ks[10], (DEPTH, D_MODEL, D_IN), D_MODEL ** -0.5),
        "conv_w": nrm(ks[11], (DEPTH, CONV_W, D_CONV), CONV_W ** -0.5),
        "a_log": jnp.log(jax.random.uniform(ks[12], (DEPTH, H_DN), F32, 1.0, 16.0)),
        "dt_bias": dt + jnp.log(-jnp.expm1(-dt)),
        "g_dn": gain(ks[13], (DEPTH, HEAD_DIM)),
        "w_o": nrm(ks[14], (DEPTH, D_MODEL, D_MODEL), D_MODEL ** -0.5),
        "rel_bias": nrm(ks[15], (N_BUCKETS, H_ATT), 0.3),
        "g_pre_mix": gain(ks[16], (DEPTH, D_MODEL)),
        "g_post_mix": gain(ks[17], (DEPTH, D_MODEL)),
        "g_pre_mlp": gain(ks[18], (DEPTH, D_MODEL)),
        "g_post_mlp": gain(ks[19], (DEPTH, D_MODEL)),
        "w_up": nrm(ks[20], (DEPTH, D_MODEL, D_FF), D_MODEL ** -0.5),
        "w_down": nrm(ks[21], (DEPTH, D_FF, D_MODEL), D_FF ** -0.5),
        "g_ple": gain(ks[22], (DEPTH, D_MODEL)),
        "w_pg": nrm(ks[23], (DEPTH, D_MODEL, D_MODEL), D_MODEL ** -0.5),
        "w_ple": nrm(ks[24], (DEPTH, D_PLE, D_MODEL), D_PLE ** -0.5),
    }


def reference(x_prompt, x_sample, p_prompt, p_sample, cache_k, cache_v, state_dn, state_conv, page_table,
              w_in, conv_w, a_log, dt_bias, g_dn, w_o, rel_bias, g_pre_mix, g_post_mix, g_pre_mlp,
              g_post_mlp, w_up, w_down, g_ple, w_pg, w_ple):
    hp, hs = x_prompt, x_sample
    kp_l, vp_l, sp_l, cp_l, ks_l, vs_l, ss_l, cs_l = [], [], [], [], [], [], [], []
    for l in range(DEPTH):
        mix_p, kp, vp, sp, cp = mixer_prompt(rmsnorm(hp, g_pre_mix[l]), w_in[l], conv_w[l], a_log[l],
                                             dt_bias[l], g_dn[l], rel_bias)
        mix_s, kn, vn, sn, cn = mixer_sample(rmsnorm(hs, g_pre_mix[l]), cache_k[l], cache_v[l], state_dn[l],
                                             state_conv[l], page_table, w_in[l], conv_w[l], a_log[l],
                                             dt_bias[l], g_dn[l], rel_bias)
        hp = block_tail(hp, mix_p, p_prompt[l], w_o[l], g_post_mix[l], g_pre_mlp[l], w_up[l], w_down[l],
                        g_post_mlp[l], g_ple[l], w_pg[l], w_ple[l])
        hs = block_tail(hs, mix_s, p_sample[l], w_o[l], g_post_mix[l], g_pre_mlp[l], w_up[l], w_down[l],
                        g_post_mlp[l], g_ple[l], w_pg[l], w_ple[l])
        kp_l.append(kp); vp_l.append(vp); sp_l.append(sp); cp_l.append(cp)
        ks_l.append(kn); vs_l.append(vn); ss_l.append(sn); cs_l.append(cn)
    return (hp, hs, jnp.stack(kp_l), jnp.stack(vp_l), jnp.stack(sp_l), jnp.stack(cp_l),
            jnp.stack(ks_l), jnp.stack(vs_l), jnp.stack(ss_l), jnp.stack(cs_l))
```

```python
import functools
import math

import numpy as np
import jax
import jax.numpy as jnp
from jax import lax
from jax.experimental import pallas as pl
from jax.experimental.pallas import tpu as pltpu

F32 = jnp.float32
BF16 = jnp.bfloat16
I32 = jnp.int32

HEAD_DIM = 64
H_DN = 8
H_ATT = 8
D_DN = H_DN * HEAD_DIM
D_ATT = H_ATT * HEAD_DIM
D_CONV = 3 * D_DN
CONV_W = 4
MOBA_BLOCK = 256
MOBA_TOPK = 3
N_BUCKETS = 32
MAX_DISTANCE = 128
PAGE_SIZE = 128
EPS = 1e-6
NEG = -1e30
LANES = 128
N_PAIR = H_DN // 2
VMEM_LIMIT = 56 * 1024 * 1024

NN = (((1,), (0,)), ((), ()))
NT = (((1,), (1,)), ((), ()))


def _dot(a, b, dims=NN):
    return lax.dot_general(a, b, dims, preferred_element_type=F32)


def _bdot(a, b, dims=NN):
    return _dot(a.astype(BF16), b.astype(BF16), dims)


def _split(a):
    hi = a.astype(BF16)
    lo = (a - hi.astype(F32)).astype(BF16)
    return hi, lo


def _dot3(a, b, dims=NN):
    ah, al = _split(a)
    bh, bl = _split(b)
    return _dot(ah, bh, dims) + (_dot(ah, bl, dims) + _dot(al, bh, dims))


def _mm(a, b, dims=NN, passes=1):
    return _bdot(a, b, dims) if passes == 1 else _dot3(a, b, dims)


def _dot_exact_lhs(a_bf16, b, dims=NN):
    b1 = b.astype(BF16)
    r1 = b - b1.astype(F32)
    b2 = r1.astype(BF16)
    b3 = (r1 - b2.astype(F32)).astype(BF16)
    return _dot(a_bf16, b1, dims) + (_dot(a_bf16, b2, dims) + _dot(a_bf16, b3, dims))


def _rms(x, g):
    return x * lax.rsqrt(jnp.mean(x * x, -1, keepdims=True) + EPS) * g


def _sigmoid(x):
    return 1.0 / (1.0 + jnp.exp(-x))


def _silu(x):
    return x * _sigmoid(x)


def _softplus(z):
    return jnp.maximum(z, 0.0) + jnp.log1p(jnp.exp(-jnp.abs(z)))


def _t5_bucket_np(n):
    n = np.asarray(n)
    max_exact = N_BUCKETS // 2
    nf = np.maximum(n, 1).astype(np.float32)
    large = max_exact + (np.log(nf / np.float32(max_exact)) / np.float32(math.log(MAX_DISTANCE / max_exact))
                         * np.float32(N_BUCKETS - max_exact)).astype(np.int32)
    large = np.minimum(large, N_BUCKETS - 1)
    return np.where(n < max_exact, n, large).astype(np.int32)


def _cparams(sem):
    return pltpu.CompilerParams(dimension_semantics=sem, vmem_limit_bytes=VMEM_LIMIT)


def _whole():
    return pl.BlockSpec(memory_space=pltpu.VMEM)


def _inproj_body(x_ref, g_ref, wm_ref, wab_ref, qkv_ref, gate_ref, ab_ref, q_ref, k_ref, v_ref, *km_refs, tile):
    x = x_ref[...]
    xn = _rms(x, g_ref[...]).astype(BF16)
    qkv_ref[...] = _dot(xn, wm_ref[:, 0:D_CONV])
    o = D_CONV
    gate_ref[...] = _dot(xn, wm_ref[:, o:o + D_DN])
    o += D_DN
    q_ref[...] = _dot(xn, wm_ref[:, o:o + D_ATT])
    o += D_ATT
    k = _dot(xn, wm_ref[:, o:o + D_ATT])
    k_ref[...] = k
    o += D_ATT
    v_ref[...] = _dot(xn, wm_ref[:, o:o + D_ATT])
    ab_ref[...] = _dot(xn, wab_ref[...])
    if km_refs:
        km_ref, = km_refs
        for i in range(tile // MOBA_BLOCK):
            km_ref[i] = jnp.mean(k[i * MOBA_BLOCK:(i + 1) * MOBA_BLOCK], axis=0, keepdims=True)


def _inproj(x, g, w_main, w_ab, tile, with_means):
    m, d = x.shape
    grid = (m // tile,)
    row = lambda w: pl.BlockSpec((tile, w), lambda i: (i, 0))
    out_shape = [jax.ShapeDtypeStruct((m, D_CONV), F32), jax.ShapeDtypeStruct((m, D_DN), F32),
                 jax.ShapeDtypeStruct((m, N_PAIR * LANES), F32), jax.ShapeDtypeStruct((m, D_ATT), F32),
                 jax.ShapeDtypeStruct((m, D_ATT), F32), jax.ShapeDtypeStruct((m, D_ATT), F32)]
    out_specs = [row(D_CONV), row(D_DN), row(N_PAIR * LANES), row(D_ATT), row(D_ATT), row(D_ATT)]
    if with_means:
        nb = tile // MOBA_BLOCK
        out_shape.append(jax.ShapeDtypeStruct((m // MOBA_BLOCK, 1, D_ATT), F32))
        out_specs.append(pl.BlockSpec((nb, 1, D_ATT), lambda i: (i, 0, 0)))
    return pl.pallas_call(
        functools.partial(_inproj_body, tile=tile),
        grid=grid,
        in_specs=[row(d), _whole(), _whole(), _whole()],
        out_specs=out_specs,
        out_shape=out_shape,
        compiler_params=_cparams(("arbitrary",)),
        name="inproj_means" if with_means else "inproj",
    )(x, g, w_main, w_ab)


DN_CHUNK = 256
DN_BASE = 16


def _pair_sum(x, lo):
    s0 = jnp.sum(jnp.where(lo, x, 0.0), -1, keepdims=True)
    s1 = jnp.sum(jnp.where(lo, 0.0, x), -1, keepdims=True)
    return jnp.where(lo, s0, s1)


def _dn_body(q_ref, k_ref, v_ref, gate_ref, ab_ref, cwq_ref, cwk_ref, cwv_ref, hpar_ref, gdn_ref,
             y_ref, sfin_ref, bq, bk, bv, s_ref, *, passes_inv, passes_mm):
    C = DN_CHUNK
    t = pl.program_id(2)
    nt = pl.num_programs(2)

    @pl.when(t == 0)
    def _():
        s_ref[...] = jnp.zeros_like(s_ref)
        for b in (bq, bk, bv):
            b[0:8, :] = jnp.zeros((8, LANES), F32)

    lane = lax.broadcasted_iota(I32, (C, LANES), 1)
    lo = lane < HEAD_DIM

    def conv(x_ref, buf, w_ref):
        x = x_ref[...]
        buf[8:8 + C, :] = x
        w = w_ref[...]
        y = (x * w[3:4] + buf[7:7 + C, :] * w[2:3] + buf[6:6 + C, :] * w[1:2] + buf[5:5 + C, :] * w[0:1])
        buf[0:8, :] = x[C - 8:C, :]
        return _silu(y)

    q = conv(q_ref, bq, cwq_ref)
    k = conv(k_ref, bk, cwk_ref)
    v = conv(v_ref, bv, cwv_ref)
    q = q * lax.rsqrt(_pair_sum(q * q, lo) + EPS) * (HEAD_DIM ** -0.5)
    k = k * lax.rsqrt(_pair_sum(k * k, lo) + EPS)

    ab = ab_ref[...]
    hpar = hpar_ref[0]
    g_all = -jnp.exp(hpar[0:1]) * _softplus(ab + hpar[1:2])
    beta_all = _sigmoid(ab)

    row = lax.broadcasted_iota(I32, (C, C), 0)
    col = lax.broadcasted_iota(I32, (C, C), 1)
    causal = row >= col
    strict = row > col
    ltri = causal.astype(BF16)
    gam_all = _dot_exact_lhs(ltri, g_all)
    ones = jnp.ones((C, LANES), BF16)
    eye = (row == col).astype(F32)

    def blk_id(idx, size):
        return lax.shift_right_logical(idx, size.bit_length() - 1)

    def head_mats(i):
        sel = lo if i == 0 else jnp.logical_not(lo)
        gcol = gam_all[:, i:i + 1]
        grow = _dot_exact_lhs(ones, jnp.where(lane == i, gam_all, 0.0), NT)
        decay = jnp.where(causal, jnp.exp(jnp.where(causal, gcol - grow, 0.0)), 0.0)
        kk = _mm(jnp.where(sel, k, 0.0), k, NT, passes_mm)
        a = jnp.where(strict, beta_all[:, 2 + i:3 + i] * kk * decay, 0.0)
        ad = jnp.where(blk_id(row, DN_BASE) == blk_id(col, DN_BASE), a, 0.0)
        p = eye - ad
        ak = ad
        for _ in range(DN_BASE.bit_length() - 2):
            ak = _mm(ak, ak, NN, passes_inv)
            p = p + _mm(p, ak, NN, passes_inv)
        size = DN_BASE
        while size < C:
            off = jnp.where(jnp.logical_and(blk_id(row, 2 * size) == blk_id(col, 2 * size),
                                            blk_id(row, size) != blk_id(col, size)), a, 0.0)
            p = p - _mm(p, _mm(off, p, NN, passes_inv), NN, passes_inv)
            size *= 2
        qk =jnp.where(causal, _mm(jnp.where(sel, q, 0.0), k, NT, passes_mm) * decay, 0.0)
        return p, qk

    gam_p = jnp.where(lo, gam_all[:, 0:1], gam_all[:, 1:2])
    beta_p = jnp.where(lo, beta_all[:, 2:3], beta_all[:, 3:4])
    lo_row = lax.broadcasted_iota(I32, (1, LANES), 1) < HEAD_DIM
    glast_p = jnp.where(lo_row, gam_all[C - 1:C, 0:1], gam_all[C - 1:C, 1:2])
    egam = jnp.exp(gam_p)
    xu = v * beta_p
    xw = k * beta_p * egam

    tinv0, qk0 = head_mats(0)
    tinv1, qk1 = head_mats(1)
    u = jnp.where(lo, _mm(tinv0, xu, NN, passes_mm), _mm(tinv1, xu, NN, passes_mm))
    w = jnp.where(lo, _mm(tinv0, xw, NN, passes_mm), _mm(tinv1, xw, NN, passes_mm))

    s = s_ref[...]
    v_new = u - _mm(w, s, NN, passes_mm)
    o = _mm(q * egam, s, NN, passes_mm) + jnp.where(lo, _mm(qk0, v_new, NN, passes_mm),
                                                   _mm(qk1, v_new, NN, passes_mm))
    kd = k * jnp.exp(glast_p - gam_p)
    r2 = lax.broadcasted_iota(I32, (LANES, LANES), 0)
    c2 = lax.broadcasted_iota(I32, (LANES, LANES), 1)
    same_head = (r2 < HEAD_DIM) == (c2 < HEAD_DIM)
    s_new = s * jnp.exp(glast_p) + jnp.where(same_head, _mm(kd.T, v_new, NN, passes_mm), 0.0)
    s_ref[...] = s_new

    ms = _pair_sum(o * o, lo) * (1.0 / HEAD_DIM)
    y_ref[...] = o * lax.rsqrt(ms + EPS) * gdn_ref[...] * _silu(gate_ref[...])

    @pl.when(t == nt - 1)
    def _():
        sfin_ref[0, 0] = s_new


def _dn_prompt(qkv_raw, gate, ab, conv_w, hpar, gdn2, batch, passes_inv=3, passes_mm=3):
    m = qkv_raw.shape[0]
    t_len = m // batch
    nt = t_len // DN_CHUNK
    C = DN_CHUNK
    blk = lambda off: pl.BlockSpec((C, LANES), lambda b, hp, t: (b * nt + t, off + hp))
    cw = lambda off: pl.BlockSpec((CONV_W, LANES), lambda b, hp, t: (0, off + hp))
    y, sfin = pl.pallas_call(
        functools.partial(_dn_body, passes_inv=passes_inv, passes_mm=passes_mm),
        grid=(batch, N_PAIR, nt),
        in_specs=[blk(0), blk(N_PAIR), blk(2 * N_PAIR), blk(0), blk(0),
                  cw(0), cw(N_PAIR), cw(2 * N_PAIR),
                  pl.BlockSpec((1, 8, LANES), lambda b, hp, t: (hp, 0, 0)),
                  pl.BlockSpec((1, LANES), lambda b, hp, t: (0, 0))],
        out_specs=[blk(0), pl.BlockSpec((1, 1, LANES, LANES), lambda b, hp, t: (b, hp, 0, 0))],
        out_shape=[jax.ShapeDtypeStruct((m, D_DN), F32), jax.ShapeDtypeStruct((batch, N_PAIR, LANES, LANES), F32)],
        scratch_shapes=[pltpu.VMEM((8 + C, LANES), F32)] * 3 + [pltpu.VMEM((LANES, LANES), F32)],
        compiler_params=_cparams(("arbitrary", "arbitrary", "arbitrary")),
        name="deltanet_prompt",
    )(qkv_raw, qkv_raw, qkv_raw, gate, ab, conv_w, conv_w, conv_w, hpar, gdn2)
    return y, sfin


def _topk_mask(s, lane, n_valid):
    cur = jnp.where(lane < n_valid, s, NEG)
    sel = jnp.zeros_like(s)
    for _ in range(MOBA_TOPK):
        mx = jnp.max(cur, -1, keepdims=True)
        first = jnp.min(jnp.where(cur == mx, lane, LANES), -1, keepdims=True)
        pick = jnp.logical_and(lane == first, mx > 0.5 * NEG)
        sel = jnp.where(pick, 1.0, sel)
        cur = jnp.where(lane == first, NEG, cur)
    return sel


def _moba_body(qi_tab, kj_tab, q_ref, k_ref, v_ref, km_ref, bkt_ref, tbl_ref, o_ref,
               bias_sc, sel_sc, m_sc, l_sc, acc_sc):
    B = MOBA_BLOCK
    hp = pl.program_id(1)
    s_id = pl.program_id(2)
    qi = qi_tab[s_id]
    kj = kj_tab[s_id]
    first = kj == qi
    last = jnp.logical_or(kj == qi - 1, qi == 0)

    lane = lax.broadcasted_iota(I32, (B, LANES), 1)
    lo = lane < HEAD_DIM
    row = lax.broadcasted_iota(I32, (B, B), 0)
    col = lax.broadcasted_iota(I32, (B, B), 1)

    @pl.when(s_id == 0)
    def _():
        for i in range(2):
            h = 2 * hp + i
            for kind in range(2):
                bk = bkt_ref[kind]
                bias_sc[i, kind] = lax.fori_loop(
                    0, N_BUCKETS, lambda b, acc: jnp.where(bk == b, tbl_ref[b, h], acc), jnp.zeros((B, B), F32))
            bias_sc[i, 2] = jnp.full((B, B), tbl_ref[N_BUCKETS - 1, h], F32)

    q = q_ref[...]

    @pl.when(first)
    def _():
        km = km_ref[0]
        for i in range(2):
            sel_h = lo if i == 0 else jnp.logical_not(lo)
            s = _dot3(jnp.where(sel_h, q, 0.0), km, NT)
            sel = _topk_mask(s, lane, qi)
            sel_sc[i] = jnp.where(lane == qi, 1.0, sel)
            m_sc[i] = jnp.full((B, 1), NEG, F32)
            l_sc[i] = jnp.zeros((B, 1), F32)
        acc_sc[...] = jnp.zeros_like(acc_sc)

    kb = k_ref[...].astype(BF16)
    vb = v_ref[...].astype(BF16)
    qs = q * (HEAD_DIM ** -0.5)
    kind = jnp.where(first, 0, jnp.where(kj == qi - 1, 1, 2))
    tri_ok = jnp.logical_or(col <= row, jnp.logical_not(first))
    alphas, pvs = [], []
    for i in range(2):
        sel_h = lo if i == 0 else jnp.logical_not(lo)
        lg = _dot(jnp.where(sel_h, qs, 0.0).astype(BF16), kb, NT)
        selcol = jnp.sum(jnp.where(lane == kj, sel_sc[i], 0.0), -1, keepdims=True)
        valid = jnp.logical_and(selcol > 0.5, tri_ok)
        lg = jnp.where(valid, lg + bias_sc[i, kind], NEG)
        m_old = m_sc[i]
        m_new = jnp.maximum(m_old, jnp.max(lg, -1, keepdims=True))
        alpha = jnp.exp(m_old - m_new)
        p = jnp.exp(lg - m_new)
        l_sc[i] = alpha * l_sc[i] + jnp.sum(p, -1, keepdims=True)
        m_sc[i] = m_new
        alphas.append(alpha)
        pvs.append(_dot(p.astype(BF16), vb))
    acc = acc_sc[...] * jnp.where(lo, alphas[0], alphas[1]) + jnp.where(lo, pvs[0], pvs[1])
    acc_sc[...] = acc

    @pl.when(last)
    def _():
        o_ref[...] = acc / jnp.where(lo, l_sc[0], l_sc[1])


def _moba_prompt(q, k, v, km_pad, bkt, rel_bias, batch):
    m = q.shape[0]
    nt = (m // batch) // MOBA_BLOCK
    qi_l, kj_l = [], []
    for qi in range(nt):
        for kj in [qi] + list(range(qi)):
            qi_l.append(qi)
            kj_l.append(kj)
    qi_tab = jnp.asarray(np.array(qi_l, np.int32))
    kj_tab = jnp.asarray(np.array(kj_l, np.int32))
    B = MOBA_BLOCK
    qspec = pl.BlockSpec((B, LANES), lambda b, hp, s, qt, kt: (b * nt + qt[s], hp))
    kspec = pl.BlockSpec((B, LANES), lambda b, hp, s, qt, kt: (b * nt + kt[s], hp))
    grid_spec = pltpu.PrefetchScalarGridSpec(
        num_scalar_prefetch=2,
        grid=(batch, N_PAIR, len(qi_l)),
        in_specs=[qspec, kspec, kspec,
                  pl.BlockSpec((1, LANES, LANES), lambda b, hp, s, qt, kt: (b, 0, hp)),
                  pl.BlockSpec((2, B, B), lambda b, hp, s, qt, kt: (0, 0, 0)),
                  pl.BlockSpec(memory_space=pltpu.SMEM)],
        out_specs=qspec,
        scratch_shapes=[pltpu.VMEM((2, 3, B, B), F32), pltpu.VMEM((2, B, LANES), F32),
                        pltpu.VMEM((2, B, 1), F32), pltpu.VMEM((2, B, 1), F32), pltpu.VMEM((B, LANES), F32)],
    )
    return pl.pallas_call(
        _moba_body,
        grid_spec=grid_spec,
        out_shape=jax.ShapeDtypeStruct((m, D_ATT), F32),
        compiler_params=_cparams(("arbitrary", "arbitrary", "arbitrary")),
        name="moba_prompt",
    )(qi_tab, kj_tab, q, k, v, km_pad, bkt, rel_bias)


FF_CHUNK = 1024


def _tail_body(x_ref, ydn_ref, oat_ref, p_ref, wo_ref, gpm_ref, gpre_ref, wup_ref, wdn_ref, gpost_ref,
               gple_ref, wpg_ref, wple_ref, o_ref):
    t = _dot(ydn_ref[...].astype(BF16), wo_ref[0:D_DN, :]) + _dot(oat_ref[...].astype(BF16), wo_ref[D_DN:, :])
    h = x_ref[...] + _rms(t, gpm_ref[...])
    hn = _rms(h, gpre_ref[...]).astype(BF16)
    d_ff = wup_ref.shape[1]
    acc = jnp.zeros_like(h)
    for c in range(d_ff // FF_CHUNK):
        u = jnp.maximum(_dot(hn, wup_ref[:, c * FF_CHUNK:(c + 1) * FF_CHUNK]), 0.0)
        acc = acc + _dot((u * u).astype(BF16), wdn_ref[c * FF_CHUNK:(c + 1) * FF_CHUNK, :])
    h = h + _rms(acc, gpost_ref[...])
    gate = _sigmoid(_dot(_rms(h, gple_ref[...]).astype(BF16), wpg_ref[...]))
    o_ref[...] = h + _dot(p_ref[...].astype(BF16), wple_ref[...]) * gate


def _tail(x, ydn, oat, p, wo, gpm, gpre, wup, wdn, gpost, gple, wpg, wple, tile):
    m, d = x.shape
    row = lambda w: pl.BlockSpec((tile, w), lambda i: (i, 0))
    return pl.pallas_call(
        _tail_body,
        grid=(m // tile,),
        in_specs=[row(d), row(D_DN), row(D_ATT), row(p.shape[1])] + [_whole()] * 9,
        out_specs=row(d),
        out_shape=jax.ShapeDtypeStruct((m, d), F32),
        compiler_params=_cparams(("arbitrary",)),
        name="block_tail",
    )(x, ydn, oat, p, wo, gpm, gpre, wup, wdn, gpost, gple, wpg, wple)


def _dn_step_body(x_ref, cst_ref, cw_ref, a_ref, b_ref, alog_ref, dtb_ref, gate_ref, gdn_ref, s_ref,
                  y_ref, cnew_ref, snew_ref):
    x = x_ref[0]
    cst = cst_ref[0]
    cw = cw_ref[...]
    y = cst[0] * cw[0] + cst[1] * cw[1] + cst[2] * cw[2] + x * cw[3]
    y = _silu(y)
    cnew_ref[0, 0] = cst[1]
    cnew_ref[0, 1] = cst[2]
    cnew_ref[0, 2] = x
    nrm = lax.rsqrt(jnp.sum(y * y, -1, keepdims=True) + EPS)
    q = y[0:H_DN] * nrm[0:H_DN] * (HEAD_DIM ** -0.5)
    k = y[H_DN:2 * H_DN] * nrm[H_DN:2 * H_DN]
    v = y[2 * H_DN:3 * H_DN]
    g = -jnp.exp(alog_ref[...]) * _softplus(a_ref[0] + dtb_ref[...])
    beta = _sigmoid(b_ref[0])
    r = lax.broadcasted_iota(I32, (HEAD_DIM, HEAD_DIM), 0)
    c = lax.broadcasted_iota(I32, (HEAD_DIM, HEAD_DIM), 1)
    eye = r == c

    def to_col(vrow):
        return jnp.sum(jnp.where(eye, jnp.broadcast_to(vrow, (HEAD_DIM, HEAD_DIM)), 0.0), -1, keepdims=True)

    gdn = gdn_ref[...]
    for h in range(H_DN):
        s = s_ref[0, h] * jnp.exp(g[h:h + 1])
        kcol = to_col(k[h:h + 1])
        err = v[h:h + 1] - jnp.sum(s * kcol, 0, keepdims=True)
        s = s + (kcol * beta[h:h + 1]) * err
        snew_ref[0, h] = s
        o = jnp.sum(s * to_col(q[h:h + 1]), 0, keepdims=True)
        y_ref[0, h:h + 1, :] = _rms(o, gdn) * _silu(gate_ref[0, h:h + 1, :])


def _dn_step(x3, cst4, cw3, a3, b3, alog, dtb, gate3, gdn, state):
    nb = x3.shape[0]
    r3 = lambda s: pl.BlockSpec((1,) + s, lambda b: (b,) + (0,) * len(s))
    return pl.pallas_call(
        _dn_step_body,
        grid=(nb,),
        in_specs=[r3((3 * H_DN, HEAD_DIM)), r3((CONV_W - 1, 3 * H_DN, HEAD_DIM)), _whole(),
                  r3((H_DN, 1)), r3((H_DN, 1)), _whole(), _whole(), r3((H_DN, HEAD_DIM)), _whole(),
                  r3((H_DN, HEAD_DIM, HEAD_DIM))],
        out_specs=[r3((H_DN, HEAD_DIM)), r3((CONV_W - 1, 3 * H_DN, HEAD_DIM)), r3((H_DN, HEAD_DIM, HEAD_DIM))],
        out_shape=[jax.ShapeDtypeStruct((nb, H_DN, HEAD_DIM), F32),
                   jax.ShapeDtypeStruct((nb, CONV_W - 1, 3 * H_DN, HEAD_DIM), F32),
                   jax.ShapeDtypeStruct((nb, H_DN, HEAD_DIM, HEAD_DIM), F32)],
        compiler_params=_cparams(("arbitrary",)),
        name="deltanet_step",
    )(x3, cst4, cw3, a3, b3, alog, dtb, gate3, gdn, state)


def _head_rows(qrow):
    r = lax.broadcasted_iota(I32, (H_ATT, D_ATT), 0)
    c = lax.broadcasted_iota(I32, (H_ATT, D_ATT), 1)
    return jnp.where(c // HEAD_DIM == r, jnp.broadcast_to(qrow, (H_ATT, D_ATT)), 0.0)


def _kpass_body(pt_ref, q_ref, kpage_ref, lg_ref, idx_ref, ksum_sc, *, n_blocks):
    p = pl.program_id(1)
    npg = pl.num_programs(1)

    @pl.when(p == 0)
    def _():
        ksum_sc[...] = jnp.zeros_like(ksum_sc)

    page = kpage_ref[0]
    qh = _head_rows(q_ref[0])
    lg_ref[0] = _bdot(qh, page, NT)
    blk = p // (MOBA_BLOCK // PAGE_SIZE)
    ksum_sc[pl.ds(blk, 1), :] = ksum_sc[pl.ds(blk, 1), :] + jnp.sum(page, 0, keepdims=True)

    @pl.when(p == npg - 1)
    def _():
        km = ksum_sc[...] * (1.0 / MOBA_BLOCK)
        s = _dot3(qh, km, NT)
        lane = lax.broadcasted_iota(I32, (H_ATT, LANES), 1)
        cur = jnp.where(lane < n_blocks, s, NEG)
        out = jnp.zeros((H_ATT, LANES), I32)
        for j in range(MOBA_TOPK):
            mx = jnp.max(cur, -1, keepdims=True)
            first = jnp.min(jnp.where(cur == mx, lane, LANES), -1, keepdims=True)
            out = jnp.where(lane == j, first, out)
            cur = jnp.where(lane == first, NEG, cur)
        idx_ref[0] = out


def _kpass(page_table, q3, cache_k3):
    nb, n_pages = page_table.shape
    n_blocks = n_pages * PAGE_SIZE // MOBA_BLOCK
    grid_spec = pltpu.PrefetchScalarGridSpec(
        num_scalar_prefetch=1,
        grid=(nb, n_pages),
        in_specs=[pl.BlockSpec((1, 1, D_ATT), lambda b, p, pt: (b, 0, 0)),
                  pl.BlockSpec((1, PAGE_SIZE, D_ATT), lambda b, p, pt: (pt[b, p], 0, 0))],
        out_specs=[pl.BlockSpec((1, H_ATT, PAGE_SIZE), lambda b, p, pt: (b, 0, p)),
                   pl.BlockSpec((1, H_ATT, LANES), lambda b, p, pt: (b, 0, 0))],
        scratch_shapes=[pltpu.VMEM((LANES, D_ATT), F32)],
    )
    return pl.pallas_call(
        functools.partial(_kpass_body, n_blocks=n_blocks),
        grid_spec=grid_spec,
        out_shape=[jax.ShapeDtypeStruct((nb, H_ATT, n_pages * PAGE_SIZE), F32),
                   jax.ShapeDtypeStruct((nb, H_ATT, LANES), I32)],
        compiler_params=_cparams(("arbitrary", "arbitrary")),
        name="moba_decode_kpass",
    )(page_table, q3, cache_k3)


def _vpass_body(pt_ref, idx_ref, lg_ref, q_ref, kn_ref, vn_ref, bkt_ref, tblt_ref, cv_ref, o_ref, vbuf, sem,
                *, n_blocks):
    b = pl.program_id(0)
    ppb = MOBA_BLOCK // PAGE_SIZE

    def copies():
        out = []
        for h in range(H_ATT):
            for j in range(MOBA_TOPK):
                blk = idx_ref[b, h, j]
                for pg in range(ppb):
                    page = pt_ref[b, blk * ppb + pg]
                    out.append(pltpu.make_async_copy(
                        cv_ref.at[page, :, pl.ds((h // 2) * LANES, LANES)],
                        vbuf.at[h, j, pl.ds(pg * PAGE_SIZE, PAGE_SIZE), :], sem.at[0]))
        return out

    cps = copies()
    for cp in cps:
        cp.start()

    bkt = bkt_ref[...]
    tblt = tblt_ref[...]
    bias_last = jnp.zeros((H_ATT, MOBA_BLOCK), F32)
    for bb in range(N_BUCKETS):
        bias_last = jnp.where(bkt == bb, tblt[:, bb:bb + 1], bias_last)
    bias_far = tblt[:, N_BUCKETS - 1:N_BUCKETS]
    scale = HEAD_DIM ** -0.5
    q = q_ref[0]
    lg_self = jnp.sum(q * kn_ref[0], -1, keepdims=True) * scale + tblt[:, 0:1]

    lgs = []
    for h in range(H_ATT):
        per = []
        for j in range(MOBA_TOPK):
            blk = idx_ref[b, h, j]
            start = pl.multiple_of(blk * MOBA_BLOCK, MOBA_BLOCK)
            raw = lg_ref[0, h:h + 1, pl.ds(start, MOBA_BLOCK)]
            bias = jnp.where(blk == n_blocks - 1, bias_last[h:h + 1], bias_far[h:h + 1])
            per.append(raw * scale + bias)
        lgs.append(per)

    for cp in cps:
        cp.wait()

    for h in range(H_ATT):
        ls = lg_self[h:h + 1]
        mx = ls
        for j in range(MOBA_TOPK):
            mx = jnp.maximum(mx, jnp.max(lgs[h][j], -1, keepdims=True))
        p_self = jnp.exp(ls - mx)
        den = p_self
        acc = p_self * vn_ref[0, h:h + 1]
        for j in range(MOBA_TOPK):
            pj = jnp.exp(lgs[h][j] - mx)
            den = den + jnp.sum(pj, -1, keepdims=True)
            pv = _bdot(pj, vbuf[h, j])
            acc = acc + pv[:, (h % 2) * HEAD_DIM:(h % 2 + 1) * HEAD_DIM]
        o_ref[0, h:h + 1, :] = acc / den


def _vpass(page_table, idx, logits, q3, kn3, vn3, bkt_last, tbl_t, cache_v3):
    nb, n_pages = page_table.shape
    n_blocks = n_pages * PAGE_SIZE // MOBA_BLOCK
    r3 = lambda s: pl.BlockSpec((1,) + s, lambda b, pt, ix: (b,) + (0,) * len(s))
    grid_spec = pltpu.PrefetchScalarGridSpec(
        num_scalar_prefetch=2,
        grid=(nb,),
        in_specs=[r3((H_ATT, n_pages * PAGE_SIZE)), r3((H_ATT, HEAD_DIM)), r3((H_ATT, HEAD_DIM)),
                  r3((H_ATT, HEAD_DIM)), _whole(), _whole(), pl.BlockSpec(memory_space=pl.ANY)],
        out_specs=r3((H_ATT, HEAD_DIM)),
        scratch_shapes=[pltpu.VMEM((H_ATT, MOBA_TOPK, MOBA_BLOCK, LANES), F32), pltpu.SemaphoreType.DMA((1,))],
    )
    return pl.pallas_call(
        functools.partial(_vpass_body, n_blocks=n_blocks),
        grid_spec=grid_spec,
        out_shape=jax.ShapeDtypeStruct((nb, H_ATT, HEAD_DIM), F32),
        compiler_params=_cparams(("arbitrary",)),
        name="moba_decode_vpass",
    )(page_table, idx, logits, q3, kn3, vn3, bkt_last, tbl_t, cache_v3)


def _pair_layout_ab(w_a, w_b):
    d = w_a.shape[0]
    out = jnp.zeros((d, N_PAIR, LANES), w_a.dtype)
    out = out.at[:, :, 0:2].set(w_a.reshape(d, N_PAIR, 2))
    out = out.at[:, :, 2:4].set(w_b.reshape(d, N_PAIR, 2))
    return out.reshape(d, N_PAIR * LANES)


def kernel(x_prompt, x_sample, p_prompt, p_sample, cache_k, cache_v, state_dn, state_conv, page_table, w_in, conv_w, a_log, dt_bias, g_dn, w_o, rel_bias, g_pre_mix, g_post_mix, g_pre_mlp, g_post_mlp, w_up, w_down, g_ple, w_pg, w_ple):
    batch, seq, d = x_prompt.shape
    nb = x_sample.shape[0]
    depth = w_in.shape[0]
    assert depth == 1 and x_sample.shape[1] == 1
    assert seq % MOBA_BLOCK == 0 and seq // MOBA_BLOCK <= LANES
    l = 0
    row = lambda a: a.reshape(1, -1)

    wi = w_in[l]
    o_gate = D_CONV + D_DN
    o_att = o_gate + 2 * H_DN
    w_main = jnp.concatenate([wi[:, :o_gate], wi[:, o_att:]], axis=1).astype(BF16)
    w_ab = _pair_layout_ab(wi[:, o_gate:o_gate + H_DN], wi[:, o_gate + H_DN:o_att]).astype(BF16)
    hpar = jnp.zeros((N_PAIR, 8, LANES), F32)
    hpar = hpar.at[:, 0, 0:2].set(a_log[l].reshape(N_PAIR, 2)).at[:, 1, 0:2].set(dt_bias[l].reshape(N_PAIR, 2))
    gdn2 = jnp.tile(g_dn[l], 2).reshape(1, LANES)
    wo, wup, wdn, wpg, wple = (w.astype(BF16) for w in (w_o[l], w_up[l], w_down[l], w_pg[l], w_ple[l]))
    tail_w = (wo, row(g_post_mix[l]), row(g_pre_mlp[l]), wup, wdn, row(g_post_mlp[l]), row(g_ple[l]), wpg, wple)

    m = batch * seq
    xp = x_prompt.reshape(m, d)
    qkv_raw, gate, ab, q_att, k_att, v_att, kmeans = _inproj(xp, row(g_pre_mix[l]), w_main, w_ab, 512, True)
    y_dn, sfin = _dn_prompt(qkv_raw, gate, ab, conv_w[l], hpar, gdn2, batch)
    nblk = seq // MOBA_BLOCK
    km_pad = jnp.zeros((batch, LANES, D_ATT), F32).at[:, :nblk].set(kmeans.reshape(batch, nblk, D_ATT))
    r_i = np.arange(MOBA_BLOCK)[:, None]
    c_i = np.arange(MOBA_BLOCK)[None, :]
    bkt = jnp.asarray(np.stack([_t5_bucket_np(np.maximum(r_i - c_i, 0)), _t5_bucket_np(MOBA_BLOCK + r_i - c_i)]))
    o_at = _moba_prompt(q_att, k_att, v_att, km_pad, bkt, rel_bias, batch)
    y_prompt = _tail(xp, y_dn, o_at, p_prompt[l].reshape(m, -1), *tail_w, tile=256).reshape(batch, seq, d)

    k_prompt = k_att.reshape(1, batch, seq, H_ATT, HEAD_DIM)
    v_prompt = v_att.reshape(1, batch, seq, H_ATT, HEAD_DIM)
    sf = sfin.reshape(batch, N_PAIR, 2, HEAD_DIM, 2, HEAD_DIM)
    dn_prompt = jnp.stack([sf[:, :, 0, :, 0, :], sf[:, :, 1, :, 1, :]], axis=2).reshape(1, batch, H_DN, HEAD_DIM, HEAD_DIM)
    conv_prompt = qkv_raw.reshape(batch, seq, D_CONV)[:, seq - (CONV_W - 1):, :].reshape(1, batch, CONV_W - 1, D_CONV)

    xs = x_sample.reshape(nb, d)
    qkv_s, gate_s, ab_s, q_s, k_s, v_s = _inproj(xs, row(g_pre_mix[l]), w_main, w_ab, nb, False)
    ab4 = ab_s.reshape(nb, N_PAIR, LANES)
    a3 = ab4[:, :, 0:2].reshape(nb, H_DN, 1)
    b3 = ab4[:, :, 2:4].reshape(nb, H_DN, 1)
    y3, cnew, snew = _dn_step(
        qkv_s.reshape(nb, 3 * H_DN, HEAD_DIM), state_conv[l].reshape(nb, CONV_W - 1, 3 * H_DN, HEAD_DIM),
        conv_w[l].reshape(CONV_W, 3 * H_DN, HEAD_DIM), a3, b3, a_log[l].reshape(H_DN, 1), dt_bias[l].reshape(H_DN, 1),
        gate_s.reshape(nb, H_DN, HEAD_DIM), row(g_dn[l]), state_dn[l])
    n_pages = page_table.shape[1]
    n_pool = cache_k.shape[1]
    ck3 = cache_k[l].reshape(n_pool, PAGE_SIZE, D_ATT)
    cv3 = cache_v[l].reshape(n_pool, PAGE_SIZE, D_ATT)
    logits, idx = _kpass(page_table, q_s.reshape(nb, 1, D_ATT), ck3)
    past = n_pages * PAGE_SIZE
    bkt_last = jnp.asarray(_t5_bucket_np(past - (past - MOBA_BLOCK + np.arange(MOBA_BLOCK))).reshape(1, MOBA_BLOCK))
    o_s = _vpass(page_table, idx[:, :, :MOBA_TOPK], logits, q_s.reshape(nb, H_ATT, HEAD_DIM),
                 k_s.reshape(nb, H_ATT, HEAD_DIM), v_s.reshape(nb, H_ATT, HEAD_DIM), bkt_last,
                 rel_bias.T, cv3)
    y_sample = _tail(xs, y3.reshape(nb, D_DN), o_s.reshape(nb, D_ATT), p_sample[l].reshape(nb, -1), *tail_w,
                     tile=nb).reshape(nb, 1, d)

    k_sample = k_s.reshape(1, nb, 1, H_ATT, HEAD_DIM)
    v_sample = v_s.reshape(1, nb, 1, H_ATT, HEAD_DIM)
    dn_sample = snew.reshape(1, nb, H_DN, HEAD_DIM, HEAD_DIM)
    conv_sample = cnew.reshape(1, nb, CONV_W - 1, D_CONV)
    return (y_prompt, y_sample, k_prompt, v_prompt, dn_prompt, conv_prompt,
            k_sample, v_sample, dn_sample, conv_sample)
```

```python
import functools
import math

import numpy as np
import jax
import jax.numpy as jnp
from jax import lax
from jax.experimental import pallas as pl
from jax.experimental.pallas import tpu as pltpu

F32 = jnp.float32
BF16 = jnp.bfloat16
I32 = jnp.int32

HEAD_DIM = 64
H_DN = 8
H_ATT = 8
D_DN = H_DN * HEAD_DIM
D_ATT = H_ATT * HEAD_DIM
D_CONV = 3 * D_DN
CONV_W = 4
MOBA_BLOCK = 256
MOBA_TOPK = 3
N_BUCKETS = 32
MAX_DISTANCE = 128
PAGE_SIZE = 128
EPS = 1e-6
NEG = -1e30
LANES = 128
N_PAIR = H_DN // 2
VMEM_LIMIT = 56 * 1024 * 1024

NN = (((1,), (0,)), ((), ()))
NT = (((1,), (1,)), ((), ()))


def _dot(a, b, dims=NN):
    return lax.dot_general(a, b, dims, preferred_element_type=F32)


def _bdot(a, b, dims=NN):
    return _dot(a.astype(BF16), b.astype(BF16), dims)


def _split(a):
    hi = a.astype(BF16)
    lo = (a - hi.astype(F32)).astype(BF16)
    return hi, lo


def _dot3(a, b, dims=NN):
    ah, al = _split(a)
    bh, bl = _split(b)
    return _dot(ah, bh, dims) + (_dot(ah, bl, dims) + _dot(al, bh, dims))


def _mm(a, b, dims=NN, passes=1):
    return _bdot(a, b, dims) if passes == 1 else _dot3(a, b, dims)


def _dot_exact_lhs(a_bf16, b, dims=NN):
    b1 = b.astype(BF16)
    r1 = b - b1.astype(F32)
    b2 = r1.astype(BF16)
    b3 = (r1 - b2.astype(F32)).astype(BF16)
    return _dot(a_bf16, b1, dims) + (_dot(a_bf16, b2, dims) + _dot(a_bf16, b3, dims))


def _rms(x, g):
    return x * lax.rsqrt(jnp.mean(x * x, -1, keepdims=True) + EPS) * g


def _sigmoid(x):
    return 1.0 / (1.0 + jnp.exp(-x))


def _silu(x):
    return x * _sigmoid(x)


def _softplus(z):
    return jnp.maximum(z, 0.0) + jnp.log1p(jnp.exp(-jnp.abs(z)))


def _t5_bucket_np(n):
    n = np.asarray(n)
    max_exact = N_BUCKETS // 2
    nf = np.maximum(n, 1).astype(np.float32)
    large = max_exact + (np.log(nf / np.float32(max_exact)) / np.float32(math.log(MAX_DISTANCE / max_exact))
                         * np.float32(N_BUCKETS - max_exact)).astype(np.int32)
    large = np.minimum(large, N_BUCKETS - 1)
    return np.where(n < max_exact, n, large).astype(np.int32)


def _cparams(sem):
    return pltpu.CompilerParams(dimension_semantics=sem, vmem_limit_bytes=VMEM_LIMIT)


def _whole():
    return pl.BlockSpec(memory_space=pltpu.VMEM)


def _inproj_body(x_ref, g_ref, wm_ref, wab_ref, qkv_ref, gate_ref, ab_ref, q_ref, k_ref, v_ref, *km_refs, tile):
    x = x_ref[...]
    xn = _rms(x, g_ref[...]).astype(BF16)
    qkv_ref[...] = _dot(xn, wm_ref[:, 0:D_CONV])
    o = D_CONV
    gate_ref[...] = _dot(xn, wm_ref[:, o:o + D_DN])
    o += D_DN
    q_ref[...] = _dot(xn, wm_ref[:, o:o + D_ATT])
    o += D_ATT
    k = _dot(xn, wm_ref[:, o:o + D_ATT])
    k_ref[...] = k
    o += D_ATT
    v_ref[...] = _dot(xn, wm_ref[:, o:o + D_ATT])
    ab_ref[...] = _dot(xn, wab_ref[...])
    if km_refs:
        km_ref, = km_refs
        for i in range(tile // MOBA_BLOCK):
            km_ref[i] = jnp.mean(k[i * MOBA_BLOCK:(i + 1) * MOBA_BLOCK], axis=0, keepdims=True)


def _inproj(x, g, w_main, w_ab, tile, with_means):
    m, d = x.shape
    grid = (m // tile,)
    row = lambda w: pl.BlockSpec((tile, w), lambda i: (i, 0))
    out_shape = [jax.ShapeDtypeStruct((m, D_CONV), F32), jax.ShapeDtypeStruct((m, D_DN), F32),
                 jax.ShapeDtypeStruct((m, N_PAIR * LANES), F32), jax.ShapeDtypeStruct((m, D_ATT), F32),
                 jax.ShapeDtypeStruct((m, D_ATT), F32), jax.ShapeDtypeStruct((m, D_ATT), F32)]
    out_specs = [row(D_CONV), row(D_DN), row(N_PAIR * LANES), row(D_ATT), row(D_ATT), row(D_ATT)]
    if with_means:
        nb = tile // MOBA_BLOCK
        out_shape.append(jax.ShapeDtypeStruct((m // MOBA_BLOCK, 1, D_ATT), F32))
        out_specs.append(pl.BlockSpec((nb, 1, D_ATT), lambda i: (i, 0, 0)))
    return pl.pallas_call(
        functools.partial(_inproj_body, tile=tile),
        grid=grid,
        in_specs=[row(d), _whole(), _whole(), _whole()],
        out_specs=out_specs,
        out_shape=out_shape,
        compiler_params=_cparams(("arbitrary",)),
        name="inproj_means" if with_means else "inproj",
    )(x, g, w_main, w_ab)


DN_CHUNK = 256
DN_BASE = 2


def _pair_sum(x, lo):
    s0 = jnp.sum(jnp.where(lo, x, 0.0), -1, keepdims=True)
    s1 = jnp.sum(jnp.where(lo, 0.0, x), -1, keepdims=True)
    return jnp.where(lo, s0, s1)


def _dn_body(q_ref, k_ref, v_ref, gate_ref, ab_ref, cwq_ref, cwk_ref, cwv_ref, hpar_ref, gdn_ref,
             y_ref, sfin_ref, bq, bk, bv, s_ref, *, passes_inv, passes_mm):
    C = DN_CHUNK
    t = pl.program_id(2)
    nt = pl.num_programs(2)

    @pl.when(t == 0)
    def _():
        s_ref[...] = jnp.zeros_like(s_ref)
        for b in (bq, bk, bv):
            b[0:8, :] = jnp.zeros((8, LANES), F32)

    lane = lax.broadcasted_iota(I32, (C, LANES), 1)
    lo = lane < HEAD_DIM

    def conv(x_ref, buf, w_ref):
        x = x_ref[...]
        buf[8:8 + C, :] = x
        w = w_ref[...]
        y = (x * w[3:4] + buf[7:7 + C, :] * w[2:3] + buf[6:6 + C, :] * w[1:2] + buf[5:5 + C, :] * w[0:1])
        buf[0:8, :] = x[C - 8:C, :]
        return _silu(y)

    q = conv(q_ref, bq, cwq_ref)
    k = conv(k_ref, bk, cwk_ref)
    v = conv(v_ref, bv, cwv_ref)
    q = q * lax.rsqrt(_pair_sum(q * q, lo) + EPS) * (HEAD_DIM ** -0.5)
    k = k * lax.rsqrt(_pair_sum(k * k, lo) + EPS)

    ab = ab_ref[...]
    hpar = hpar_ref[0]
    g_all = -jnp.exp(hpar[0:1]) * _softplus(ab + hpar[1:2])
    beta_all = _sigmoid(ab)

    HB = C // 2
    row = lax.broadcasted_iota(I32, (C, C), 0)
    col = lax.broadcasted_iota(I32, (C, C), 1)
    causal = row >= col
    strict = row > col
    rh = lax.broadcasted_iota(I32, (HB, HB), 0)
    ch = lax.broadcasted_iota(I32, (HB, HB), 1)
    ltri = (rh >= ch).astype(BF16)
    eye = (rh == ch).astype(F32)
    gam_top = _dot_exact_lhs(ltri, g_all[0:HB])
    gam_all = jnp.concatenate([gam_top, _dot_exact_lhs(ltri, g_all[HB:]) + gam_top[HB - 1:HB]], axis=0)
    gam_t = gam_all.T

    def blk_id(idx, size):
        return lax.shift_right_logical(idx, size.bit_length() - 1)

    def inv_halves(blocks):
        ps = [eye - jnp.where(blk_id(rh, DN_BASE) == blk_id(ch, DN_BASE), a, 0.0) for a in blocks]
        size = DN_BASE
        while size < HB:
            pick = jnp.logical_and(blk_id(rh, 2 * size) == blk_id(ch, 2 * size),
                                   blk_id(rh, size) != blk_id(ch, size))
            ws = [_mm(jnp.where(pick, a, 0.0), p, NN, passes_inv) for a, p in zip(blocks, ps)]
            ps = [p - _mm(p, w, NN, passes_inv) for p, w in zip(ps, ws)]
            size *= 2
        return ps

    def head_pre(i):
        sel = lo if i == 0 else jnp.logical_not(lo)
        decay = jnp.where(causal, jnp.exp(jnp.where(causal, gam_all[:, i:i + 1] - gam_t[i:i + 1, :], 0.0)), 0.0)
        kk = _mm(jnp.where(sel, k, 0.0), k, NT, passes_mm)
        a = jnp.where(strict, beta_all[:, 2 + i:3 + i] * kk * decay, 0.0)
        qk = jnp.where(causal, _mm(jnp.where(sel, q, 0.0), k, NT, passes_mm) * decay, 0.0)
        return a, qk

    a0, qk0 = head_pre(0)
    a1, qk1 = head_pre(1)
    p00, p01, p10, p11 = inv_halves([a0[0:HB, 0:HB], a0[HB:, HB:], a1[0:HB, 0:HB], a1[HB:, HB:]])
    y0 = _mm(a0[HB:, 0:HB], p00, NN, passes_inv)
    y1 = _mm(a1[HB:, 0:HB], p10, NN, passes_inv)
    t0, b0 = p00, jnp.concatenate([-_mm(p01, y0, NN, passes_inv), p01], axis=1)
    t1, b1 = p10, jnp.concatenate([-_mm(p11, y1, NN, passes_inv), p11], axis=1)

    gam_p = jnp.where(lo, gam_all[:, 0:1], gam_all[:, 1:2])
    beta_p = jnp.where(lo, beta_all[:, 2:3], beta_all[:, 3:4])
    lo_row = lax.broadcasted_iota(I32, (1, LANES), 1) < HEAD_DIM
    glast_p = jnp.where(lo_row, gam_all[C - 1:C, 0:1], gam_all[C - 1:C, 1:2])
    egam = jnp.exp(gam_p)
    xuw = jnp.concatenate([v * beta_p, k * beta_p * egam], axis=1)

    def apply_rows(top, bot, x):
        return jnp.concatenate([_mm(top, x[0:HB], NN, passes_mm), _mm(bot, x, NN, passes_mm)], axis=0)

    uw0 = apply_rows(t0, b0, xuw)
    uw1 = apply_rows(t1, b1, xuw)
    u = jnp.where(lo, uw0[:, 0:LANES], uw1[:, 0:LANES])
    w = jnp.where(lo, uw0[:, LANES:], uw1[:, LANES:])

    s = s_ref[...]
    v_new = u - _mm(w, s, NN, passes_mm)
    o = _mm(q * egam, s, NN, passes_mm) + jnp.where(lo, apply_rows(qk0[0:HB, 0:HB], qk0[HB:], v_new),
                                                   apply_rows(qk1[0:HB, 0:HB], qk1[HB:], v_new))
    kd = k * jnp.exp(glast_p - gam_p)
    r2 = lax.broadcasted_iota(I32, (LANES, LANES), 0)
    c2 = lax.broadcasted_iota(I32, (LANES, LANES), 1)
    same_head = (r2 < HEAD_DIM) == (c2 < HEAD_DIM)
    s_new = s * jnp.exp(glast_p) + jnp.where(same_head, _mm(kd.T, v_new, NN, passes_mm), 0.0)
    s_ref[...] = s_new

    ms = _pair_sum(o * o, lo) * (1.0 / HEAD_DIM)
    y_ref[...] = o * lax.rsqrt(ms + EPS) * gdn_ref[...] * _silu(gate_ref[...])

    @pl.when(t == nt - 1)
    def _():
        sfin_ref[0, 0] = s_new


def _dn_prompt(qkv_raw, gate, ab, conv_w, hpar, gdn2, batch, passes_inv=1, passes_mm=1):
    m = qkv_raw.shape[0]
    t_len = m // batch
    nt = t_len // DN_CHUNK
    C = DN_CHUNK
    blk = lambda off: pl.BlockSpec((C, LANES), lambda b, hp, t: (b * nt + t, off + hp))
    cw = lambda off: pl.BlockSpec((CONV_W, LANES), lambda b, hp, t: (0, off + hp))
    y, sfin = pl.pallas_call(
        functools.partial(_dn_body, passes_inv=passes_inv, passes_mm=passes_mm),
        grid=(batch, N_PAIR, nt),
        in_specs=[blk(0), blk(N_PAIR), blk(2 * N_PAIR), blk(0), blk(0),
                  cw(0), cw(N_PAIR), cw(2 * N_PAIR),
                  pl.BlockSpec((1, 8, LANES), lambda b, hp, t: (hp, 0, 0)),
                  pl.BlockSpec((1, LANES), lambda b, hp, t: (0, 0))],
        out_specs=[blk(0), pl.BlockSpec((1, 1, LANES, LANES), lambda b, hp, t: (b, hp, 0, 0))],
        out_shape=[jax.ShapeDtypeStruct((m, D_DN), F32), jax.ShapeDtypeStruct((batch, N_PAIR, LANES, LANES), F32)],
        scratch_shapes=[pltpu.VMEM((8 + C, LANES), F32)] * 3 + [pltpu.VMEM((LANES, LANES), F32)],
        compiler_params=_cparams(("arbitrary", "arbitrary", "arbitrary")),
        name="deltanet_prompt",
    )(qkv_raw, qkv_raw, qkv_raw, gate, ab, conv_w, conv_w, conv_w, hpar, gdn2)
    return y, sfin


MOBA_G = 4
MOBA_KV = MOBA_G * MOBA_BLOCK
MOBA_CHUNK = 128
MOBA_GROUP = MOBA_KV // MOBA_CHUNK
MOBA_ONES = 16
LOG2E = math.log2(math.e)
_BUCKET_START = [int(np.argmax(_t5_bucket_np(np.arange(4 * MAX_DISTANCE)) >= b)) for b in range(1, N_BUCKETS)]


def _moba_body(qi_tab, t_tab, kind_tab, q_ref, k_ref, vt_ref, km_ref, tbl_ref, o_ref,
               bias_sc, qaug_sc, m_sc, acc_sc, st_sc, *, nblk_pad):
    B, KV = MOBA_BLOCK, MOBA_KV
    hp = pl.program_id(1)
    s_id = pl.program_id(2)
    qi = qi_tab[s_id]
    t = t_tab[s_id]
    kind = kind_tab[s_id]
    t_own = qi // MOBA_G
    first = t == t_own
    last = jnp.logical_or(t == t_own - 1, t_own == 0)

    @pl.when(s_id == 0)
    def _():
        c = lax.broadcasted_iota(I32, (B, B), 0)
        r = lax.broadcasted_iota(I32, (B, B), 1)
        for i in range(2):
            h = 2 * hp + i
            far = tbl_ref[N_BUCKETS - 1, h]
            for j in range(2):
                dist = r + j * B - c
                acc = jnp.full((B, B), tbl_ref[0, h], F32)
                for b in range(1, N_BUCKETS):
                    acc = jnp.where(dist >= _BUCKET_START[b - 1], tbl_ref[b, h], acc)
                bias_sc[i, j] = jnp.where(dist < 0, NEG, (acc - far) * LOG2E)
            bias_sc[i, 2] = jnp.zeros((B, B), F32)

    lane = lax.broadcasted_iota(I32, (B, LANES), 1)
    lo = lane < HEAD_DIM

    @pl.when(first)
    def _():
        q = q_ref[...]
        km = km_ref[0]
        nrow = lax.broadcasted_iota(I32, (nblk_pad, B), 0)
        for i in range(2):
            head = lo if i == 0 else jnp.logical_not(lo)
            st = _dot3(km, jnp.where(head, q, 0.0), NT)[0:nblk_pad]
            cur = jnp.where(nrow < qi, st, NEG)
            sel = nrow == qi
            for _ in range(MOBA_TOPK):
                mx = jnp.max(cur, 0, keepdims=True)
                pick = nrow == jnp.min(jnp.where(cur == mx, nrow, nblk_pad), 0, keepdims=True)
                sel = jnp.logical_or(sel, jnp.logical_and(pick, mx > 0.5 * NEG))
                cur = jnp.where(pick, NEG, cur)
            xt = jnp.where(sel, 0.0, NEG)
            off = HEAD_DIM if i == 0 else 0
            parts = [xt, jnp.zeros((LANES - off - nblk_pad, B), F32)]
            if off:
                parts = [jnp.zeros((off, B), F32)] + parts
            x = jnp.concatenate(parts, axis=0).T
            qaug_sc[i] = jnp.where(head, q * (HEAD_DIM ** -0.5 * LOG2E), x).astype(BF16)
            m_sc[i] = jnp.full((1, B), NEG, F32)
            acc_sc[i] = jnp.zeros((HEAD_DIM + MOBA_ONES, B), F32)

    CH = MOBA_CHUNK
    lane1 = lax.broadcasted_iota(I32, (1, LANES), 1)
    klo = lax.broadcasted_iota(I32, (CH, LANES), 1) < HEAD_DIM

    def step(with_bias):
        ones = jnp.ones((MOBA_ONES, CH), BF16)
        m_run = [m_sc[i] for i in range(2)]
        for grp in range(KV // (CH * MOBA_GROUP)):
            cks = range(grp * MOBA_GROUP, (grp + 1) * MOBA_GROUP)
            mx8 = [None, None]
            for ck in cks:
                kb = k_ref[pl.ds(ck * CH, CH), :].astype(BF16)
                for i in range(2):
                    blk_lane = t * MOBA_G + (ck * CH) // MOBA_BLOCK + (HEAD_DIM if i == 0 else 0)
                    yrow = jnp.where(lane1 == blk_lane, 1.0, 0.0).astype(BF16)
                    kaug = jnp.where(klo, kb, yrow) if i == 0 else jnp.where(klo, yrow, kb)
                    s = _dot(kaug, qaug_sc[i], NT)
                    if with_bias:
                        dq = qi - (t * MOBA_G + (ck * CH) // MOBA_BLOCK)
                        which = jnp.where(dq == 0, 0, jnp.where(dq == 1, 1, 2))
                        s = s + bias_sc[i, which, pl.ds((ck * CH) % MOBA_BLOCK, CH), :]
                    s3 = s.reshape(CH // 8, 8, B)
                    st_sc[i, pl.ds(ck * (CH // 8), CH // 8)] = s3
                    cm = jnp.max(s3, axis=0)
                    mx8[i] = cm if mx8[i] is None else jnp.maximum(mx8[i], cm)
            m_new = [jnp.maximum(m_run[i], jnp.max(mx8[i], 0, keepdims=True)) for i in range(2)]
            m8 = [jnp.broadcast_to(m_new[i], (8, B)) for i in range(2)]
            pv = [None, None]
            for ck in cks:
                for i in range(2):
                    p3 = jnp.exp2(st_sc[i, pl.ds(ck * (CH // 8), CH // 8)] - m8[i][None])
                    p = p3.reshape(CH, B).astype(BF16)
                    vaug = jnp.concatenate([vt_ref[i * HEAD_DIM:(i + 1) * HEAD_DIM, pl.ds(ck * CH, CH)], ones],
                                           axis=0)
                    d = _dot(vaug, p)
                    pv[i] = d if pv[i] is None else pv[i] + d
            for i in range(2):
                acc_sc[i] = acc_sc[i] * jnp.exp2(m_run[i] - m_new[i]) + pv[i]
            m_run = m_new
        for i in range(2):
            m_sc[i] = m_run[i]

    @pl.when(kind > 0)
    def _():
        step(True)

    @pl.when(kind == 0)
    def _():
        step(False)

    @pl.when(last)
    def _():
        a0 = acc_sc[0]
        a1 = acc_sc[1]
        ot = jnp.concatenate([a0[0:HEAD_DIM] / a0[HEAD_DIM:HEAD_DIM + 1],
                              a1[0:HEAD_DIM] / a1[HEAD_DIM:HEAD_DIM + 1]], axis=0)
        o_ref[...] = ot.T


def _moba_prompt(q, k, vt, km_pad, rel_bias, batch):
    m = q.shape[0]
    nt = (m // batch) // MOBA_BLOCK
    ntile = nt // MOBA_G
    nblk_pad = -(-nt // 8) * 8
    qi_l, t_l, kind_l = [], [], []
    for qi in range(nt):
        t_own = qi // MOBA_G
        for t in [t_own] + list(range(t_own)):
            dq = qi - t * MOBA_G
            qi_l.append(qi)
            t_l.append(t)
            kind_l.append(1 if dq <= MOBA_G else 0)
    tabs = [jnp.asarray(np.array(a, np.int32)) for a in (qi_l, t_l, kind_l)]
    B, KV = MOBA_BLOCK, MOBA_KV
    qspec = pl.BlockSpec((B, LANES), lambda b, hp, s, qt, tt, kt: (b * nt + qt[s], hp))
    grid_spec = pltpu.PrefetchScalarGridSpec(
        num_scalar_prefetch=3,
        grid=(batch, N_PAIR, len(qi_l)),
        in_specs=[qspec,
                  pl.BlockSpec((KV, LANES), lambda b, hp, s, qt, tt, kt: (b * ntile + tt[s], hp)),
                  pl.BlockSpec((LANES, KV), lambda b, hp, s, qt, tt, kt: (hp, b * ntile + tt[s])),
                  pl.BlockSpec((1, LANES, LANES), lambda b, hp, s, qt, tt, kt: (b, 0, hp)),
                  pl.BlockSpec(memory_space=pltpu.SMEM)],
        out_specs=qspec,
        scratch_shapes=[pltpu.VMEM((2, 3, B, B), F32), pltpu.VMEM((2, B, LANES), BF16),
                        pltpu.VMEM((2, 1, B), F32), pltpu.VMEM((2, HEAD_DIM + MOBA_ONES, B), F32),
                        pltpu.VMEM((2, KV // 8, 8, B), F32)],
    )
    return pl.pallas_call(
        functools.partial(_moba_body, nblk_pad=nblk_pad),
        grid_spec=grid_spec,
        out_shape=jax.ShapeDtypeStruct((m, D_ATT), F32),
        compiler_params=_cparams(("arbitrary", "arbitrary", "arbitrary")),
        name="moba_prompt",
    )(*tabs, q, k, vt, km_pad, rel_bias)


FF_CHUNK = 1024


def _tail_body(x_ref, ydn_ref, oat_ref, p_ref, wo_ref, gpm_ref, gpre_ref, wup_ref, wdn_ref, gpost_ref,
               gple_ref, wpg_ref, wple_ref, o_ref):
    t = _dot(ydn_ref[...].astype(BF16), wo_ref[0:D_DN, :]) + _dot(oat_ref[...].astype(BF16), wo_ref[D_DN:, :])
    h = x_ref[...] + _rms(t, gpm_ref[...])
    hn = _rms(h, gpre_ref[...]).astype(BF16)
    d_ff = wup_ref.shape[1]
    acc = jnp.zeros_like(h)
    for c in range(d_ff // FF_CHUNK):
        u = jnp.maximum(_dot(hn, wup_ref[:, c * FF_CHUNK:(c + 1) * FF_CHUNK]), 0.0)
        acc = acc + _dot((u * u).astype(BF16), wdn_ref[c * FF_CHUNK:(c + 1) * FF_CHUNK, :])
    h = h + _rms(acc, gpost_ref[...])
    gate = _sigmoid(_dot(_rms(h, gple_ref[...]).astype(BF16), wpg_ref[...]))
    o_ref[...] = h + _dot(p_ref[...].astype(BF16), wple_ref[...]) * gate


def _tail(x, ydn, oat, p, wo, gpm, gpre, wup, wdn, gpost, gple, wpg, wple, tile):
    m, d = x.shape
    row = lambda w: pl.BlockSpec((tile, w), lambda i: (i, 0))
    return pl.pallas_call(
        _tail_body,
        grid=(m // tile,),
        in_specs=[row(d), row(D_DN), row(D_ATT), row(p.shape[1])] + [_whole()] * 9,
        out_specs=row(d),
        out_shape=jax.ShapeDtypeStruct((m, d), F32),
        compiler_params=_cparams(("arbitrary",)),
        name="block_tail",
    )(x, ydn, oat, p, wo, gpm, gpre, wup, wdn, gpost, gple, wpg, wple)


def _dn_step_body(x_ref, cst_ref, cw_ref, a_ref, b_ref, alog_ref, dtb_ref, gate_ref, gdn_ref, s_ref,
                  y_ref, cnew_ref, snew_ref):
    x = x_ref[0]
    cst = cst_ref[0]
    cw = cw_ref[...]
    y = cst[0] * cw[0] + cst[1] * cw[1] + cst[2] * cw[2] + x * cw[3]
    y = _silu(y)
    cnew_ref[0, 0] = cst[1]
    cnew_ref[0, 1] = cst[2]
    cnew_ref[0, 2] = x
    nrm = lax.rsqrt(jnp.sum(y * y, -1, keepdims=True) + EPS)
    q = y[0:H_DN] * nrm[0:H_DN] * (HEAD_DIM ** -0.5)
    k = y[H_DN:2 * H_DN] * nrm[H_DN:2 * H_DN]
    v = y[2 * H_DN:3 * H_DN]
    g = -jnp.exp(alog_ref[...]) * _softplus(a_ref[0] + dtb_ref[...])
    beta = _sigmoid(b_ref[0])
    r = lax.broadcasted_iota(I32, (HEAD_DIM, HEAD_DIM), 0)
    c = lax.broadcasted_iota(I32, (HEAD_DIM, HEAD_DIM), 1)
    eye = r == c

    def to_col(vrow):
        return jnp.sum(jnp.where(eye, jnp.broadcast_to(vrow, (HEAD_DIM, HEAD_DIM)), 0.0), -1, keepdims=True)

    gdn = gdn_ref[...]
    for h in range(H_DN):
        s = s_ref[0, h] * jnp.exp(g[h:h + 1])
        kcol = to_col(k[h:h + 1])
        err = v[h:h + 1] - jnp.sum(s * kcol, 0, keepdims=True)
        s = s + (kcol * beta[h:h + 1]) * err
        snew_ref[0, h] = s
        o = jnp.sum(s * to_col(q[h:h + 1]), 0, keepdims=True)
        y_ref[0, h:h + 1, :] = _rms(o, gdn) * _silu(gate_ref[0, h:h + 1, :])


def _dn_step(x3, cst4, cw3, a3, b3, alog, dtb, gate3, gdn, state):
    nb = x3.shape[0]
    r3 = lambda s: pl.BlockSpec((1,) + s, lambda b: (b,) + (0,) * len(s))
    return pl.pallas_call(
        _dn_step_body,
        grid=(nb,),
        in_specs=[r3((3 * H_DN, HEAD_DIM)), r3((CONV_W - 1, 3 * H_DN, HEAD_DIM)), _whole(),
                  r3((H_DN, 1)), r3((H_DN, 1)), _whole(), _whole(), r3((H_DN, HEAD_DIM)), _whole(),
                  r3((H_DN, HEAD_DIM, HEAD_DIM))],
        out_specs=[r3((H_DN, HEAD_DIM)), r3((CONV_W - 1, 3 * H_DN, HEAD_DIM)), r3((H_DN, HEAD_DIM, HEAD_DIM))],
        out_shape=[jax.ShapeDtypeStruct((nb, H_DN, HEAD_DIM), F32),
                   jax.ShapeDtypeStruct((nb, CONV_W - 1, 3 * H_DN, HEAD_DIM), F32),
                   jax.ShapeDtypeStruct((nb, H_DN, HEAD_DIM, HEAD_DIM), F32)],
        compiler_params=_cparams(("arbitrary",)),
        name="deltanet_step",
    )(x3, cst4, cw3, a3, b3, alog, dtb, gate3, gdn, state)


def _head_rows(qrow):
    r = lax.broadcasted_iota(I32, (H_ATT, D_ATT), 0)
    c = lax.broadcasted_iota(I32, (H_ATT, D_ATT), 1)
    return jnp.where(c // HEAD_DIM == r, jnp.broadcast_to(qrow, (H_ATT, D_ATT)), 0.0)


def _kpass_body(pt_ref, q_ref, kpage_ref, lg_ref, idx_ref, ksum_sc, *, n_blocks):
    p = pl.program_id(1)
    npg = pl.num_programs(1)

    @pl.when(p == 0)
    def _():
        ksum_sc[...] = jnp.zeros_like(ksum_sc)

    page = kpage_ref[0]
    qh = _head_rows(q_ref[0])
    lg_ref[0] = _bdot(qh, page, NT)
    blk = p // (MOBA_BLOCK // PAGE_SIZE)
    ksum_sc[pl.ds(blk, 1), :] = ksum_sc[pl.ds(blk, 1), :] + jnp.sum(page, 0, keepdims=True)

    @pl.when(p == npg - 1)
    def _():
        km = ksum_sc[...] * (1.0 / MOBA_BLOCK)
        s = _dot3(qh, km, NT)
        lane = lax.broadcasted_iota(I32, (H_ATT, LANES), 1)
        cur = jnp.where(lane < n_blocks, s, NEG)
        out = jnp.zeros((H_ATT, LANES), I32)
        for j in range(MOBA_TOPK):
            mx = jnp.max(cur, -1, keepdims=True)
            first = jnp.min(jnp.where(cur == mx, lane, LANES), -1, keepdims=True)
            out = jnp.where(lane == j, first, out)
            cur = jnp.where(lane == first, NEG, cur)
        idx_ref[0] = out


def _kpass(page_table, q3, cache_k3):
    nb, n_pages = page_table.shape
    n_blocks = n_pages * PAGE_SIZE // MOBA_BLOCK
    grid_spec = pltpu.PrefetchScalarGridSpec(
        num_scalar_prefetch=1,
        grid=(nb, n_pages),
        in_specs=[pl.BlockSpec((1, 1, D_ATT), lambda b, p, pt: (b, 0, 0)),
                  pl.BlockSpec((1, PAGE_SIZE, D_ATT), lambda b, p, pt: (pt[b, p], 0, 0))],
        out_specs=[pl.BlockSpec((1, H_ATT, PAGE_SIZE), lambda b, p, pt: (b, 0, p)),
                   pl.BlockSpec((1, H_ATT, LANES), lambda b, p, pt: (b, 0, 0))],
        scratch_shapes=[pltpu.VMEM((LANES, D_ATT), F32)],
    )
    return pl.pallas_call(
        functools.partial(_kpass_body, n_blocks=n_blocks),
        grid_spec=grid_spec,
        out_shape=[jax.ShapeDtypeStruct((nb, H_ATT, n_pages * PAGE_SIZE), F32),
                   jax.ShapeDtypeStruct((nb, H_ATT, LANES), I32)],
        compiler_params=_cparams(("arbitrary", "arbitrary")),
        name="moba_decode_kpass",
    )(page_table, q3, cache_k3)


def _vpass_body(pt_ref, idx_ref, lg_ref, q_ref, kn_ref, vn_ref, bkt_ref, tblt_ref, cv_ref, o_ref, vbuf, sem,
                *, n_blocks):
    b = pl.program_id(0)
    ppb = MOBA_BLOCK // PAGE_SIZE

    def copies():
        out = []
        for h in range(H_ATT):
            for j in range(MOBA_TOPK):
                blk = idx_ref[b, h, j]
                for pg in range(ppb):
                    page = pt_ref[b, blk * ppb + pg]
                    out.append(pltpu.make_async_copy(
                        cv_ref.at[page, :, pl.ds((h // 2) * LANES, LANES)],
                        vbuf.at[h, j, pl.ds(pg * PAGE_SIZE, PAGE_SIZE), :], sem.at[0]))
        return out

    cps = copies()
    for cp in cps:
        cp.start()

    bkt = bkt_ref[...]
    tblt = tblt_ref[...]
    bias_last = jnp.zeros((H_ATT, MOBA_BLOCK), F32)
    for bb in range(N_BUCKETS):
        bias_last = jnp.where(bkt == bb, tblt[:, bb:bb + 1], bias_last)
    bias_far = tblt[:, N_BUCKETS - 1:N_BUCKETS]
    scale = HEAD_DIM ** -0.5
    q = q_ref[0]
    lg_self = jnp.sum(q * kn_ref[0], -1, keepdims=True) * scale + tblt[:, 0:1]

    lgs = []
    for h in range(H_ATT):
        per = []
        for j in range(MOBA_TOPK):
            blk = idx_ref[b, h, j]
            start = pl.multiple_of(blk * MOBA_BLOCK, MOBA_BLOCK)
            raw = lg_ref[0, h:h + 1, pl.ds(start, MOBA_BLOCK)]
            bias = jnp.where(blk == n_blocks - 1, bias_last[h:h + 1], bias_far[h:h + 1])
            per.append(raw * scale + bias)
        lgs.append(per)

    for cp in cps:
        cp.wait()

    for h in range(H_ATT):
        ls = lg_self[h:h + 1]
        mx = ls
        for j in range(MOBA_TOPK):
            mx = jnp.maximum(mx, jnp.max(lgs[h][j], -1, keepdims=True))
        p_self = jnp.exp(ls - mx)
        den = p_self
        acc = p_self * vn_ref[0, h:h + 1]
        for j in range(MOBA_TOPK):
            pj = jnp.exp(lgs[h][j] - mx)
            den = den + jnp.sum(pj, -1, keepdims=True)
            pv = _bdot(pj, vbuf[h, j])
            acc = acc + pv[:, (h % 2) * HEAD_DIM:(h % 2 + 1) * HEAD_DIM]
        o_ref[0, h:h + 1, :] = acc / den


def _vpass(page_table, idx, logits, q3, kn3, vn3, bkt_last, tbl_t, cache_v3):
    nb, n_pages = page_table.shape
    n_blocks = n_pages * PAGE_SIZE // MOBA_BLOCK
    r3 = lambda s: pl.BlockSpec((1,) + s, lambda b, pt, ix: (b,) + (0,) * len(s))
    grid_spec = pltpu.PrefetchScalarGridSpec(
        num_scalar_prefetch=2,
        grid=(nb,),
        in_specs=[r3((H_ATT, n_pages * PAGE_SIZE)), r3((H_ATT, HEAD_DIM)), r3((H_ATT, HEAD_DIM)),
                  r3((H_ATT, HEAD_DIM)), _whole(), _whole(), pl.BlockSpec(memory_space=pl.ANY)],
        out_specs=r3((H_ATT, HEAD_DIM)),
        scratch_shapes=[pltpu.VMEM((H_ATT, MOBA_TOPK, MOBA_BLOCK, LANES), F32), pltpu.SemaphoreType.DMA((1,))],
    )
    return pl.pallas_call(
        functools.partial(_vpass_body, n_blocks=n_blocks),
        grid_spec=grid_spec,
        out_shape=jax.ShapeDtypeStruct((nb, H_ATT, HEAD_DIM), F32),
        compiler_params=_cparams(("arbitrary",)),
        name="moba_decode_vpass",
    )(page_table, idx, logits, q3, kn3, vn3, bkt_last, tbl_t, cache_v3)


def _pair_layout_ab(w_a, w_b):
    d = w_a.shape[0]
    out = jnp.zeros((d, N_PAIR, LANES), w_a.dtype)
    out = out.at[:, :, 0:2].set(w_a.reshape(d, N_PAIR, 2))
    out = out.at[:, :, 2:4].set(w_b.reshape(d, N_PAIR, 2))
    return out.reshape(d, N_PAIR * LANES)


def kernel(x_prompt, x_sample, p_prompt, p_sample, cache_k, cache_v, state_dn, state_conv, page_table, w_in, conv_w, a_log, dt_bias, g_dn, w_o, rel_bias, g_pre_mix, g_post_mix, g_pre_mlp, g_post_mlp, w_up, w_down, g_ple, w_pg, w_ple):
    batch, seq, d = x_prompt.shape
    nb = x_sample.shape[0]
    depth = w_in.shape[0]
    assert depth == 1 and x_sample.shape[1] == 1
    assert seq % MOBA_KV == 0 and seq // MOBA_BLOCK <= LANES - HEAD_DIM
    assert page_table.shape[1] * PAGE_SIZE // MOBA_BLOCK >= MOBA_TOPK
    l = 0
    row = lambda a: a.reshape(1, -1)

    wi = w_in[l]
    o_gate = D_CONV + D_DN
    o_att = o_gate + 2 * H_DN
    w_main = jnp.concatenate([wi[:, :o_gate], wi[:, o_att:]], axis=1).astype(BF16)
    w_ab = _pair_layout_ab(wi[:, o_gate:o_gate + H_DN], wi[:, o_gate + H_DN:o_att]).astype(BF16)
    hpar = jnp.zeros((N_PAIR, 8, LANES), F32)
    hpar = hpar.at[:, 0, 0:2].set(a_log[l].reshape(N_PAIR, 2)).at[:, 1, 0:2].set(dt_bias[l].reshape(N_PAIR, 2))
    gdn2 = jnp.tile(g_dn[l], 2).reshape(1, LANES)
    wo, wup, wdn, wpg, wple = (w.astype(BF16) for w in (w_o[l], w_up[l], w_down[l], w_pg[l], w_ple[l]))
    tail_w = (wo, row(g_post_mix[l]), row(g_pre_mlp[l]), wup, wdn, row(g_post_mlp[l]), row(g_ple[l]), wpg, wple)

    m = batch * seq
    xp = x_prompt.reshape(m, d)
    qkv_raw, gate, ab, q_att, k_att, v_att, kmeans = _inproj(xp, row(g_pre_mix[l]), w_main, w_ab, 512, True)
    y_dn, sfin = _dn_prompt(qkv_raw, gate, ab, conv_w[l], hpar, gdn2, batch)
    nblk = seq // MOBA_BLOCK
    km_pad = jnp.zeros((batch, LANES, D_ATT), F32).at[:, :nblk].set(kmeans.reshape(batch, nblk, D_ATT))
    o_at = _moba_prompt(q_att, k_att, v_att.T.astype(BF16), km_pad, rel_bias, batch)
    y_prompt = _tail(xp, y_dn, o_at, p_prompt[l].reshape(m, -1), *tail_w, tile=256).reshape(batch, seq, d)

    k_prompt = k_att.reshape(1, batch, seq, H_ATT, HEAD_DIM)
    v_prompt = v_att.reshape(1, batch, seq, H_ATT, HEAD_DIM)
    sf = sfin.reshape(batch, N_PAIR, 2, HEAD_DIM, 2, HEAD_DIM)
    dn_prompt = jnp.stack([sf[:, :, 0, :, 0, :], sf[:, :, 1, :, 1, :]], axis=2).reshape(1, batch, H_DN, HEAD_DIM, HEAD_DIM)
    conv_prompt = qkv_raw.reshape(batch, seq, D_CONV)[:, seq - (CONV_W - 1):, :].reshape(1, batch, CONV_W - 1, D_CONV)

    xs = x_sample.reshape(nb, d)
    qkv_s, gate_s, ab_s, q_s, k_s, v_s = _inproj(xs, row(g_pre_mix[l]), w_main, w_ab, nb, False)
    ab4 = ab_s.reshape(nb, N_PAIR, LANES)
    a3 = ab4[:, :, 0:2].reshape(nb, H_DN, 1)
    b3 = ab4[:, :, 2:4].reshape(nb, H_DN, 1)
    y3, cnew, snew = _dn_step(
        qkv_s.reshape(nb, 3 * H_DN, HEAD_DIM), state_conv[l].reshape(nb, CONV_W - 1, 3 * H_DN, HEAD_DIM),
        conv_w[l].reshape(CONV_W, 3 * H_DN, HEAD_DIM), a3, b3, a_log[l].reshape(H_DN, 1), dt_bias[l].reshape(H_DN, 1),
        gate_s.reshape(nb, H_DN, HEAD_DIM), row(g_dn[l]), state_dn[l])
    n_pages = page_table.shape[1]
    n_pool = cache_k.shape[1]
    ck3 = cache_k[l].reshape(n_pool, PAGE_SIZE, D_ATT)
    cv3 = cache_v[l].reshape(n_pool, PAGE_SIZE, D_ATT)
    logits, idx = _kpass(page_table, q_s.reshape(nb, 1, D_ATT), ck3)
    past = n_pages * PAGE_SIZE
    bkt_last = jnp.asarray(_t5_bucket_np(past - (past - MOBA_BLOCK + np.arange(MOBA_BLOCK))).reshape(1, MOBA_BLOCK))
    o_s = _vpass(page_table, idx[:, :, :MOBA_TOPK], logits, q_s.reshape(nb, H_ATT, HEAD_DIM),
                 k_s.reshape(nb, H_ATT, HEAD_DIM), v_s.reshape(nb, H_ATT, HEAD_DIM), bkt_last,
                 rel_bias.T, cv3)
    y_sample = _tail(xs, y3.reshape(nb, D_DN), o_s.reshape(nb, D_ATT), p_sample[l].reshape(nb, -1), *tail_w,
                     tile=nb).reshape(nb, 1, d)

    k_sample = k_s.reshape(1, nb, 1, H_ATT, HEAD_DIM)
    v_sample = v_s.reshape(1, nb, 1, H_ATT, HEAD_DIM)
    dn_sample = snew.reshape(1, nb, H_DN, HEAD_DIM, HEAD_DIM)
    conv_sample = cnew.reshape(1, nb, CONV_W - 1, D_CONV)
    return (y_prompt, y_sample, k_prompt, v_prompt, dn_prompt, conv_prompt,
            k_sample, v_sample, dn_sample, conv_sample)
```

```python
import functools
import math

import numpy as np
import jax
import jax.numpy as jnp
from jax import lax
from jax.experimental import pallas as pl
from jax.experimental.pallas import tpu as pltpu

F32 = jnp.float32
BF16 = jnp.bfloat16
I32 = jnp.int32

HEAD_DIM = 64
H_DN = 8
H_ATT = 8
D_DN = H_DN * HEAD_DIM
D_ATT = H_ATT * HEAD_DIM
D_CONV = 3 * D_DN
CONV_W = 4
MOBA_BLOCK = 256
MOBA_TOPK = 3
N_BUCKETS = 32
MAX_DISTANCE = 128
PAGE_SIZE = 128
EPS = 1e-6
NEG = -1e30
LANES = 128
N_PAIR = H_DN // 2
VMEM_LIMIT = 56 * 1024 * 1024

NN = (((1,), (0,)), ((), ()))
NT = (((1,), (1,)), ((), ()))


def _dot(a, b, dims=NN):
    return lax.dot_general(a, b, dims, preferred_element_type=F32)


def _bdot(a, b, dims=NN):
    return _dot(a.astype(BF16), b.astype(BF16), dims)


def _split(a):
    hi = a.astype(BF16)
    lo = (a - hi.astype(F32)).astype(BF16)
    return hi, lo


def _dot3(a, b, dims=NN):
    ah, al = _split(a)
    bh, bl = _split(b)
    return _dot(ah, bh, dims) + (_dot(ah, bl, dims) + _dot(al, bh, dims))


def _mm(a, b, dims=NN, passes=1):
    return _bdot(a, b, dims) if passes == 1 else _dot3(a, b, dims)


def _dot_exact_lhs(a_bf16, b, dims=NN):
    b1 = b.astype(BF16)
    r1 = b - b1.astype(F32)
    b2 = r1.astype(BF16)
    b3 = (r1 - b2.astype(F32)).astype(BF16)
    return _dot(a_bf16, b1, dims) + (_dot(a_bf16, b2, dims) + _dot(a_bf16, b3, dims))


def _rms(x, g):
    return x * lax.rsqrt(jnp.mean(x * x, -1, keepdims=True) + EPS) * g


def _sigmoid(x):
    return 1.0 / (1.0 + jnp.exp(-x))


def _silu(x):
    return x * _sigmoid(x)


def _softplus(z):
    return jnp.maximum(z, 0.0) + jnp.log1p(jnp.exp(-jnp.abs(z)))


def _t5_bucket_np(n):
    n = np.asarray(n)
    max_exact = N_BUCKETS // 2
    nf = np.maximum(n, 1).astype(np.float32)
    large = max_exact + (np.log(nf / np.float32(max_exact)) / np.float32(math.log(MAX_DISTANCE / max_exact))
                         * np.float32(N_BUCKETS - max_exact)).astype(np.int32)
    large = np.minimum(large, N_BUCKETS - 1)
    return np.where(n < max_exact, n, large).astype(np.int32)


def _cparams(sem):
    return pltpu.CompilerParams(dimension_semantics=sem, vmem_limit_bytes=VMEM_LIMIT)


def _whole():
    return pl.BlockSpec(memory_space=pltpu.VMEM)


def _inproj_body(x_ref, g_ref, wm_ref, wab_ref, qkv_ref, gate_ref, ab_ref, q_ref, k_ref, v_ref, *km_refs, tile):
    x = x_ref[...]
    xn = _rms(x, g_ref[...]).astype(BF16)
    qkv_ref[...] = _dot(xn, wm_ref[:, 0:D_CONV])
    o = D_CONV
    gate_ref[...] = _dot(xn, wm_ref[:, o:o + D_DN])
    o += D_DN
    q_ref[...] = _dot(xn, wm_ref[:, o:o + D_ATT])
    o += D_ATT
    k = _dot(xn, wm_ref[:, o:o + D_ATT])
    k_ref[...] = k
    o += D_ATT
    v_ref[...] = _dot(xn, wm_ref[:, o:o + D_ATT])
    ab_ref[...] = _dot(xn, wab_ref[...])
    if km_refs:
        km_ref, = km_refs
        for i in range(tile // MOBA_BLOCK):
            km_ref[i] = jnp.mean(k[i * MOBA_BLOCK:(i + 1) * MOBA_BLOCK], axis=0, keepdims=True)


def _inproj(x, g, w_main, w_ab, tile, with_means):
    m, d = x.shape
    grid = (m // tile,)
    row = lambda w: pl.BlockSpec((tile, w), lambda i: (i, 0))
    out_shape = [jax.ShapeDtypeStruct((m, D_CONV), F32), jax.ShapeDtypeStruct((m, D_DN), F32),
                 jax.ShapeDtypeStruct((m, N_PAIR * LANES), F32), jax.ShapeDtypeStruct((m, D_ATT), F32),
                 jax.ShapeDtypeStruct((m, D_ATT), F32), jax.ShapeDtypeStruct((m, D_ATT), F32)]
    out_specs = [row(D_CONV), row(D_DN), row(N_PAIR * LANES), row(D_ATT), row(D_ATT), row(D_ATT)]
    if with_means:
        nb = tile // MOBA_BLOCK
        out_shape.append(jax.ShapeDtypeStruct((m // MOBA_BLOCK, 1, D_ATT), F32))
        out_specs.append(pl.BlockSpec((nb, 1, D_ATT), lambda i: (i, 0, 0)))
    return pl.pallas_call(
        functools.partial(_inproj_body, tile=tile),
        grid=grid,
        in_specs=[row(d), _whole(), _whole(), _whole()],
        out_specs=out_specs,
        out_shape=out_shape,
        compiler_params=_cparams(("arbitrary",)),
        name="inproj_means" if with_means else "inproj",
    )(x, g, w_main, w_ab)


DN_CHUNK = 256
DN_BASE = 2


def _pair_sum(x, lo):
    s0 = jnp.sum(jnp.where(lo, x, 0.0), -1, keepdims=True)
    s1 = jnp.sum(jnp.where(lo, 0.0, x), -1, keepdims=True)
    return jnp.where(lo, s0, s1)


def _dn_body(q_ref, k_ref, v_ref, gate_ref, ab_ref, cwq_ref, cwk_ref, cwv_ref, hpar_ref, gdn_ref,
             y_ref, sfin_ref, bq, bk, bv, s_ref, *, passes_inv, passes_mm):
    C = DN_CHUNK
    t = pl.program_id(2)
    nt = pl.num_programs(2)

    @pl.when(t == 0)
    def _():
        s_ref[...] = jnp.zeros_like(s_ref)
        for b in (bq, bk, bv):
            b[0:8, :] = jnp.zeros((8, LANES), F32)

    lane = lax.broadcasted_iota(I32, (C, LANES), 1)
    lo = lane < HEAD_DIM

    def conv(x_ref, buf, w_ref):
        x = x_ref[...]
        buf[8:8 + C, :] = x
        w = w_ref[...]
        y = (x * w[3:4] + buf[7:7 + C, :] * w[2:3] + buf[6:6 + C, :] * w[1:2] + buf[5:5 + C, :] * w[0:1])
        buf[0:8, :] = x[C - 8:C, :]
        return _silu(y)

    q = conv(q_ref, bq, cwq_ref)
    k = conv(k_ref, bk, cwk_ref)
    v = conv(v_ref, bv, cwv_ref)
    q = q * lax.rsqrt(_pair_sum(q * q, lo) + EPS) * (HEAD_DIM ** -0.5)
    k = k * lax.rsqrt(_pair_sum(k * k, lo) + EPS)

    ab = ab_ref[...]
    hpar = hpar_ref[0]
    g_all = -jnp.exp(hpar[0:1]) * _softplus(ab + hpar[1:2])
    beta_all = _sigmoid(ab)

    HB = C // 2
    row = lax.broadcasted_iota(I32, (C, C), 0)
    col = lax.broadcasted_iota(I32, (C, C), 1)
    causal = row >= col
    strict = row > col
    rh = lax.broadcasted_iota(I32, (HB, HB), 0)
    ch = lax.broadcasted_iota(I32, (HB, HB), 1)
    ltri = (rh >= ch).astype(BF16)
    eye = (rh == ch).astype(F32)
    gam_top = _dot_exact_lhs(ltri, g_all[0:HB])
    gam_all = jnp.concatenate([gam_top, _dot_exact_lhs(ltri, g_all[HB:]) + gam_top[HB - 1:HB]], axis=0)
    gam_t = gam_all.T

    def blk_id(idx, size):
        return lax.shift_right_logical(idx, size.bit_length() - 1)

    def inv_halves(blocks):
        ps = [eye - jnp.where(blk_id(rh, DN_BASE) == blk_id(ch, DN_BASE), a, 0.0) for a in blocks]
        size = DN_BASE
        while size < HB:
            pick = jnp.logical_and(blk_id(rh, 2 * size) == blk_id(ch, 2 * size),
                                   blk_id(rh, size) != blk_id(ch, size))
            ws = [_mm(jnp.where(pick, a, 0.0), p, NN, passes_inv) for a, p in zip(blocks, ps)]
            ps = [p - _mm(p, w, NN, passes_inv) for p, w in zip(ps, ws)]
            size *= 2
        return ps

    def head_pre(i):
        sel = lo if i == 0 else jnp.logical_not(lo)
        decay = jnp.where(causal, jnp.exp(jnp.where(causal, gam_all[:, i:i + 1] - gam_t[i:i + 1, :], 0.0)), 0.0)
        kk = _mm(jnp.where(sel, k, 0.0), k, NT, passes_mm)
        a = jnp.where(strict, beta_all[:, 2 + i:3 + i] * kk * decay, 0.0)
        qk = jnp.where(causal, _mm(jnp.where(sel, q, 0.0), k, NT, passes_mm) * decay, 0.0)
        return a, qk

    a0, qk0 = head_pre(0)
    a1, qk1 = head_pre(1)
    p00, p01, p10, p11 = inv_halves([a0[0:HB, 0:HB], a0[HB:, HB:], a1[0:HB, 0:HB], a1[HB:, HB:]])
    y0 = _mm(a0[HB:, 0:HB], p00, NN, passes_inv)
    y1 = _mm(a1[HB:, 0:HB], p10, NN, passes_inv)
    t0, b0 = p00, jnp.concatenate([-_mm(p01, y0, NN, passes_inv), p01], axis=1)
    t1, b1 = p10, jnp.concatenate([-_mm(p11, y1, NN, passes_inv), p11], axis=1)

    gam_p = jnp.where(lo, gam_all[:, 0:1], gam_all[:, 1:2])
    beta_p = jnp.where(lo, beta_all[:, 2:3], beta_all[:, 3:4])
    lo_row = lax.broadcasted_iota(I32, (1, LANES), 1) < HEAD_DIM
    glast_p = jnp.where(lo_row, gam_all[C - 1:C, 0:1], gam_all[C - 1:C, 1:2])
    egam = jnp.exp(gam_p)
    xuw = jnp.concatenate([v * beta_p, k * beta_p * egam], axis=1)

    def apply_rows(top, bot, x):
        return jnp.concatenate([_mm(top, x[0:HB], NN, passes_mm), _mm(bot, x, NN, passes_mm)], axis=0)

    uw0 = apply_rows(t0, b0, xuw)
    uw1 = apply_rows(t1, b1, xuw)
    u = jnp.where(lo, uw0[:, 0:LANES], uw1[:, 0:LANES])
    w = jnp.where(lo, uw0[:, LANES:], uw1[:, LANES:])

    s = s_ref[...]
    v_new = u - _mm(w, s, NN, passes_mm)
    o = _mm(q * egam, s, NN, passes_mm) + jnp.where(lo, apply_rows(qk0[0:HB, 0:HB], qk0[HB:], v_new),
                                                   apply_rows(qk1[0:HB, 0:HB], qk1[HB:], v_new))
    kd = k * jnp.exp(glast_p - gam_p)
    r2 = lax.broadcasted_iota(I32, (LANES, LANES), 0)
    c2 = lax.broadcasted_iota(I32, (LANES, LANES), 1)
    same_head = (r2 < HEAD_DIM) == (c2 < HEAD_DIM)
    s_new = s * jnp.exp(glast_p) + jnp.where(same_head, _mm(kd.T, v_new, NN, passes_mm), 0.0)
    s_ref[...] = s_new

    ms = _pair_sum(o * o, lo) * (1.0 / HEAD_DIM)
    y_ref[...] = o * lax.rsqrt(ms + EPS) * gdn_ref[...] * _silu(gate_ref[...])

    @pl.when(t == nt - 1)
    def _():
        sfin_ref[0, 0] = s_new


def _dn_prompt(qkv_raw, gate, ab, conv_w, hpar, gdn2, batch, passes_inv=1, passes_mm=1):
    m = qkv_raw.shape[0]
    t_len = m // batch
    nt = t_len // DN_CHUNK
    C = DN_CHUNK
    blk = lambda off: pl.BlockSpec((C, LANES), lambda b, hp, t: (b * nt + t, off + hp))
    cw = lambda off: pl.BlockSpec((CONV_W, LANES), lambda b, hp, t: (0, off + hp))
    y, sfin = pl.pallas_call(
        functools.partial(_dn_body, passes_inv=passes_inv, passes_mm=passes_mm),
        grid=(batch, N_PAIR, nt),
        in_specs=[blk(0), blk(N_PAIR), blk(2 * N_PAIR), blk(0), blk(0),
                  cw(0), cw(N_PAIR), cw(2 * N_PAIR),
                  pl.BlockSpec((1, 8, LANES), lambda b, hp, t: (hp, 0, 0)),
                  pl.BlockSpec((1, LANES), lambda b, hp, t: (0, 0))],
        out_specs=[blk(0), pl.BlockSpec((1, 1, LANES, LANES), lambda b, hp, t: (b, hp, 0, 0))],
        out_shape=[jax.ShapeDtypeStruct((m, D_DN), F32), jax.ShapeDtypeStruct((batch, N_PAIR, LANES, LANES), F32)],
        scratch_shapes=[pltpu.VMEM((8 + C, LANES), F32)] * 3 + [pltpu.VMEM((LANES, LANES), F32)],
        compiler_params=_cparams(("arbitrary", "arbitrary", "arbitrary")),
        name="deltanet_prompt",
    )(qkv_raw, qkv_raw, qkv_raw, gate, ab, conv_w, conv_w, conv_w, hpar, gdn2)
    return y, sfin


MOBA_G = 4
MOBA_KV = MOBA_G * MOBA_BLOCK
MOBA_CHUNK = 128
MOBA_ONES = 16
LOG2E = math.log2(math.e)
_BUCKET_START = [int(np.argmax(_t5_bucket_np(np.arange(4 * MAX_DISTANCE)) >= b)) for b in range(1, N_BUCKETS)]


def _moba_body(qi_tab, t_tab, kind_tab, q_ref, k_ref, vt_ref, km_ref, tbl_ref, o_ref,
               bias_sc, qaug_sc, m_sc, acc_sc, st_sc, *, nblk_pad, nbatch):
    B, KV = MOBA_BLOCK, MOBA_KV
    hp = pl.program_id(0)
    s_id = pl.program_id(1)
    qi = qi_tab[s_id]
    t = t_tab[s_id]
    kind = kind_tab[s_id]
    t_own = qi // MOBA_G
    first = t == t_own
    last = jnp.logical_or(t == t_own - 1, t_own == 0)

    @pl.when(s_id == 0)
    def _():
        c = lax.broadcasted_iota(I32, (B, B), 0)
        r = lax.broadcasted_iota(I32, (B, B), 1)
        for i in range(2):
            h = 2 * hp + i
            far = tbl_ref[N_BUCKETS - 1, h]
            for j in range(2):
                dist = r + j * B - c
                acc = jnp.full((B, B), tbl_ref[0, h], F32)
                for b in range(1, N_BUCKETS):
                    acc = jnp.where(dist >= _BUCKET_START[b - 1], tbl_ref[b, h], acc)
                bias_sc[i, j] = jnp.where(dist < 0, NEG, (acc - far) * LOG2E)
            bias_sc[i, 2] = jnp.zeros((B, B), F32)

    lane = lax.broadcasted_iota(I32, (B, LANES), 1)
    lo = lane < HEAD_DIM

    @pl.when(first)
    def _():
        nrow = lax.broadcasted_iota(I32, (nblk_pad, B), 0)
        for bb in range(nbatch):
            q = q_ref[bb]
            km = km_ref[bb]
            for i in range(2):
                head = lo if i == 0 else jnp.logical_not(lo)
                st = _dot3(km, jnp.where(head, q, 0.0), NT)[0:nblk_pad]
                cur = jnp.where(nrow < qi, st, NEG)
                sel = nrow == qi
                for _ in range(MOBA_TOPK):
                    mx = jnp.max(cur, 0, keepdims=True)
                    pick = nrow == jnp.min(jnp.where(cur == mx, nrow, nblk_pad), 0, keepdims=True)
                    sel = jnp.logical_or(sel, jnp.logical_and(pick, mx > 0.5 * NEG))
                    cur = jnp.where(pick, NEG, cur)
                xt = jnp.where(sel, 0.0, NEG)
                off = HEAD_DIM if i == 0 else 0
                parts = [xt, jnp.zeros((LANES - off - nblk_pad, B), F32)]
                if off:
                    parts = [jnp.zeros((off, B), F32)] + parts
                x = jnp.concatenate(parts, axis=0).T
                qaug_sc[bb, i] = jnp.where(head, q * (HEAD_DIM ** -0.5 * LOG2E), x).astype(BF16)
                m_sc[bb, i] = jnp.full((1, B), NEG, F32)
                acc_sc[bb, i] = jnp.zeros((HEAD_DIM + MOBA_ONES, B), F32)

    CH = MOBA_CHUNK
    n_ck = KV // CH
    lane1 = lax.broadcasted_iota(I32, (1, LANES), 1)
    klo = lax.broadcasted_iota(I32, (CH, LANES), 1) < HEAD_DIM
    probs = [(bb, i) for bb in range(nbatch) for i in range(2)]

    def step(with_bias):
        ones = jnp.ones((MOBA_ONES, CH), BF16)

        def p1_chunk(bb, i, ck, mx):
            kb = k_ref[bb, pl.ds(ck * CH, CH), :].astype(BF16)
            blk_lane = t * MOBA_G + (ck * CH) // MOBA_BLOCK + (HEAD_DIM if i == 0 else 0)
            yrow = jnp.where(lane1 == blk_lane, 1.0, 0.0).astype(BF16)
            kaug = jnp.where(klo, kb, yrow) if i == 0 else jnp.where(klo, yrow, kb)
            s = _dot(kaug, qaug_sc[bb, i], NT)
            if with_bias:
                dq = qi - (t * MOBA_G + (ck * CH) // MOBA_BLOCK)
                which = jnp.where(dq == 0, 0, jnp.where(dq == 1, 1, 2))
                s = s + bias_sc[i, which, pl.ds((ck * CH) % MOBA_BLOCK, CH), :]
            s3 = s.reshape(CH // 8, 8, B)
            st_sc[bb, i, pl.ds(ck * (CH // 8), CH // 8)] = s3
            cm = jnp.max(s3, axis=0)
            return cm if mx is None else jnp.maximum(mx, cm)

        def p1_done(bb, i, mx):
            m_old = m_sc[bb, i]
            m_new = jnp.maximum(m_old, jnp.max(mx, 0, keepdims=True))
            m_sc[bb, i] = m_new
            return jnp.exp2(m_old - m_new), jnp.broadcast_to(m_new, (8, B))

        def p2_chunk(bb, i, ck, m8, pv):
            p3 = jnp.exp2(st_sc[bb, i, pl.ds(ck * (CH // 8), CH // 8)] - m8[None])
            p = p3.reshape(CH, B).astype(BF16)
            vaug = jnp.concatenate([vt_ref[bb, i * HEAD_DIM:(i + 1) * HEAD_DIM, pl.ds(ck * CH, CH)], ones], axis=0)
            d = _dot(vaug, p)
            return d if pv is None else pv + d

        prev = None
        for cur in probs + [None]:
            mx, pv = None, None
            for ck in range(n_ck):
                if cur is not None:
                    mx = p1_chunk(*cur, ck, mx)
                if prev is not None:
                    pv = p2_chunk(*prev, ck, prev_m8, pv)
            if prev is not None:
                acc_sc[prev] = acc_sc[prev] * prev_alpha + pv
            if cur is not None:
                prev_alpha, prev_m8 = p1_done(*cur, mx)
            prev = cur

    @pl.when(kind > 0)
    def _():
        step(True)

    @pl.when(kind == 0)
    def _():
        step(False)

    @pl.when(last)
    def _():
        for bb in range(nbatch):
            a0 = acc_sc[bb, 0]
            a1 = acc_sc[bb, 1]
            ot = jnp.concatenate([a0[0:HEAD_DIM] / a0[HEAD_DIM:HEAD_DIM + 1],
                                  a1[0:HEAD_DIM] / a1[HEAD_DIM:HEAD_DIM + 1]], axis=0)
            o_ref[bb] = ot.T


def _moba_prompt(q, k, vt, km_pad, rel_bias):
    batch, seq, _ = q.shape
    nt = seq // MOBA_BLOCK
    nblk_pad = -(-nt // 8) * 8
    qi_l, t_l, kind_l = [], [], []
    for qi in range(nt):
        t_own = qi // MOBA_G
        for t in [t_own] + list(range(t_own)):
            dq = qi - t * MOBA_G
            qi_l.append(qi)
            t_l.append(t)
            kind_l.append(1 if dq <= MOBA_G else 0)
    tabs = [jnp.asarray(np.array(a, np.int32)) for a in (qi_l, t_l, kind_l)]
    B, KV = MOBA_BLOCK, MOBA_KV
    qspec = pl.BlockSpec((batch, B, LANES), lambda hp, s, qt, tt, kt: (0, qt[s], hp))
    grid_spec = pltpu.PrefetchScalarGridSpec(
        num_scalar_prefetch=3,
        grid=(N_PAIR, len(qi_l)),
        in_specs=[qspec,
                  pl.BlockSpec((batch, KV, LANES), lambda hp, s, qt, tt, kt: (0, tt[s], hp)),
                  pl.BlockSpec((batch, LANES, KV), lambda hp, s, qt, tt, kt: (0, hp, tt[s])),
                  pl.BlockSpec((batch, LANES, LANES), lambda hp, s, qt, tt, kt: (0, 0, hp)),
                  pl.BlockSpec(memory_space=pltpu.SMEM)],
        out_specs=qspec,
        scratch_shapes=[pltpu.VMEM((2, 3, B, B), F32), pltpu.VMEM((batch, 2, B, LANES), BF16),
                        pltpu.VMEM((batch, 2, 1, B), F32), pltpu.VMEM((batch, 2, HEAD_DIM + MOBA_ONES, B), F32),
                        pltpu.VMEM((batch, 2, KV // 8, 8, B), F32)],
    )
    return pl.pallas_call(
        functools.partial(_moba_body, nblk_pad=nblk_pad, nbatch=batch),
        grid_spec=grid_spec,
        out_shape=jax.ShapeDtypeStruct((batch, seq, D_ATT), F32),
        compiler_params=_cparams(("arbitrary", "arbitrary")),
        name="moba_prompt",
    )(*tabs, q, k, vt, km_pad, rel_bias)


FF_CHUNK = 1024


def _tail_body(x_ref, ydn_ref, oat_ref, p_ref, wo_ref, gpm_ref, gpre_ref, wup_ref, wdn_ref, gpost_ref,
               gple_ref, wpg_ref, wple_ref, o_ref):
    t = _dot(ydn_ref[...].astype(BF16), wo_ref[0:D_DN, :]) + _dot(oat_ref[...].astype(BF16), wo_ref[D_DN:, :])
    h = x_ref[...] + _rms(t, gpm_ref[...])
    hn = _rms(h, gpre_ref[...]).astype(BF16)
    d_ff = wup_ref.shape[1]
    acc = jnp.zeros_like(h)
    for c in range(d_ff // FF_CHUNK):
        u = jnp.maximum(_dot(hn, wup_ref[:, c * FF_CHUNK:(c + 1) * FF_CHUNK]), 0.0)
        acc = acc + _dot((u * u).astype(BF16), wdn_ref[c * FF_CHUNK:(c + 1) * FF_CHUNK, :])
    h = h + _rms(acc, gpost_ref[...])
    gate = _sigmoid(_dot(_rms(h, gple_ref[...]).astype(BF16), wpg_ref[...]))
    o_ref[...] = h + _dot(p_ref[...].astype(BF16), wple_ref[...]) * gate


def _tail(x, ydn, oat, p, wo, gpm, gpre, wup, wdn, gpost, gple, wpg, wple, tile):
    m, d = x.shape
    row = lambda w: pl.BlockSpec((tile, w), lambda i: (i, 0))
    return pl.pallas_call(
        _tail_body,
        grid=(m // tile,),
        in_specs=[row(d), row(D_DN), row(D_ATT), row(p.shape[1])] + [_whole()] * 9,
        out_specs=row(d),
        out_shape=jax.ShapeDtypeStruct((m, d), F32),
        compiler_params=_cparams(("arbitrary",)),
        name="block_tail",
    )(x, ydn, oat, p, wo, gpm, gpre, wup, wdn, gpost, gple, wpg, wple)


def _dn_step_body(x_ref, cst_ref, cw_ref, a_ref, b_ref, alog_ref, dtb_ref, gate_ref, gdn_ref, s_ref,
                  y_ref, cnew_ref, snew_ref):
    x = x_ref[0]
    cst = cst_ref[0]
    cw = cw_ref[...]
    y = cst[0] * cw[0] + cst[1] * cw[1] + cst[2] * cw[2] + x * cw[3]
    y = _silu(y)
    cnew_ref[0, 0] = cst[1]
    cnew_ref[0, 1] = cst[2]
    cnew_ref[0, 2] = x
    nrm = lax.rsqrt(jnp.sum(y * y, -1, keepdims=True) + EPS)
    q = y[0:H_DN] * nrm[0:H_DN] * (HEAD_DIM ** -0.5)
    k = y[H_DN:2 * H_DN] * nrm[H_DN:2 * H_DN]
    v = y[2 * H_DN:3 * H_DN]
    g = -jnp.exp(alog_ref[...]) * _softplus(a_ref[0] + dtb_ref[...])
    beta = _sigmoid(b_ref[0])
    r = lax.broadcasted_iota(I32, (HEAD_DIM, HEAD_DIM), 0)
    c = lax.broadcasted_iota(I32, (HEAD_DIM, HEAD_DIM), 1)
    eye = r == c

    def to_col(vrow):
        return jnp.sum(jnp.where(eye, jnp.broadcast_to(vrow, (HEAD_DIM, HEAD_DIM)), 0.0), -1, keepdims=True)

    gdn = gdn_ref[...]
    for h in range(H_DN):
        s = s_ref[0, h] * jnp.exp(g[h:h + 1])
        kcol = to_col(k[h:h + 1])
        err = v[h:h + 1] - jnp.sum(s * kcol, 0, keepdims=True)
        s = s + (kcol * beta[h:h + 1]) * err
        snew_ref[0, h] = s
        o = jnp.sum(s * to_col(q[h:h + 1]), 0, keepdims=True)
        y_ref[0, h:h + 1, :] = _rms(o, gdn) * _silu(gate_ref[0, h:h + 1, :])


def _dn_step(x3, cst4, cw3, a3, b3, alog, dtb, gate3, gdn, state):
    nb = x3.shape[0]
    r3 = lambda s: pl.BlockSpec((1,) + s, lambda b: (b,) + (0,) * len(s))
    return pl.pallas_call(
        _dn_step_body,
        grid=(nb,),
        in_specs=[r3((3 * H_DN, HEAD_DIM)), r3((CONV_W - 1, 3 * H_DN, HEAD_DIM)), _whole(),
                  r3((H_DN, 1)), r3((H_DN, 1)), _whole(), _whole(), r3((H_DN, HEAD_DIM)), _whole(),
                  r3((H_DN, HEAD_DIM, HEAD_DIM))],
        out_specs=[r3((H_DN, HEAD_DIM)), r3((CONV_W - 1, 3 * H_DN, HEAD_DIM)), r3((H_DN, HEAD_DIM, HEAD_DIM))],
        out_shape=[jax.ShapeDtypeStruct((nb, H_DN, HEAD_DIM), F32),
                   jax.ShapeDtypeStruct((nb, CONV_W - 1, 3 * H_DN, HEAD_DIM), F32),
                   jax.ShapeDtypeStruct((nb, H_DN, HEAD_DIM, HEAD_DIM), F32)],
        compiler_params=_cparams(("arbitrary",)),
        name="deltanet_step",
    )(x3, cst4, cw3, a3, b3, alog, dtb, gate3, gdn, state)


PAGE_ROWS = PAGE_SIZE * H_ATT
BLOCK_ROWS = MOBA_BLOCK * H_ATT
PAGES_PER_BLOCK = MOBA_BLOCK // PAGE_SIZE
KP_PAGES = 8


def _kpass_body(pt_ref, q_ref, *refs, n_blocks):
    kpages = refs[:KP_PAGES]
    lg_ref, idx_ref, ksum_sc = refs[KP_PAGES:]
    g = pl.program_id(1)
    ng = pl.num_programs(1)

    @pl.when(g == 0)
    def _():
        ksum_sc[...] = jnp.zeros_like(ksum_sc)

    q = q_ref[0]
    qb = q.astype(BF16)
    psums = []
    for j in range(KP_PAGES):
        kp = kpages[j][0]
        lg_ref[0, :, j * PAGE_ROWS:(j + 1) * PAGE_ROWS] = _dot(qb, kp.astype(BF16), NT)
        psums.append(jnp.sum(kp.reshape(PAGE_SIZE, H_ATT, HEAD_DIM), axis=0))
    for jb in range(KP_PAGES // PAGES_PER_BLOCK):
        bsum = psums[jb * PAGES_PER_BLOCK]
        for pg in range(1, PAGES_PER_BLOCK):
            bsum = bsum + psums[jb * PAGES_PER_BLOCK + pg]
        blk = g * (KP_PAGES // PAGES_PER_BLOCK) + jb
        ksum_sc[pl.ds(pl.multiple_of(blk * H_ATT, H_ATT), H_ATT), :] = bsum

    @pl.when(g == ng - 1)
    def _():
        km = ksum_sc[...] * (1.0 / MOBA_BLOCK)
        s = _dot3(q, km, NT)
        nl = s.shape[1]
        lane = lax.broadcasted_iota(I32, (H_ATT, nl), 1)
        mine = jnp.logical_and((lane & (H_ATT - 1)) == lax.broadcasted_iota(I32, (H_ATT, nl), 0),
                               lane < n_blocks * H_ATT)
        cur = jnp.where(mine, s, NEG)
        out = jnp.zeros((H_ATT, LANES), I32)
        olane = lax.broadcasted_iota(I32, (H_ATT, LANES), 1)
        for j in range(MOBA_TOPK):
            mx = jnp.max(cur, -1, keepdims=True)
            first = jnp.min(jnp.where(cur == mx, lane, nl), -1, keepdims=True)
            out = jnp.where(olane == j, lax.shift_right_logical(first, H_ATT.bit_length() - 1), out)
            cur = jnp.where(lane == first, NEG, cur)
        idx_ref[0] = out


def _kpass(page_table, q3, cache_k2):
    nb, n_pages = page_table.shape
    n_blocks = n_pages // PAGES_PER_BLOCK
    sum_rows = -(-n_blocks * H_ATT // LANES) * LANES
    page_spec = lambda j: pl.BlockSpec((1, PAGE_ROWS, HEAD_DIM), lambda b, g, pt: (pt[b, g * KP_PAGES + j], 0, 0))
    grid_spec = pltpu.PrefetchScalarGridSpec(
        num_scalar_prefetch=1,
        grid=(nb, n_pages // KP_PAGES),
        in_specs=[pl.BlockSpec((1, H_ATT, HEAD_DIM), lambda b, g, pt: (b, 0, 0))]
        + [page_spec(j) for j in range(KP_PAGES)],
        out_specs=[pl.BlockSpec((1, H_ATT, KP_PAGES * PAGE_ROWS), lambda b, g, pt: (b, 0, g)),
                   pl.BlockSpec((1, H_ATT, LANES), lambda b, g, pt: (b, 0, 0))],
        scratch_shapes=[pltpu.VMEM((sum_rows, HEAD_DIM), F32)],
    )
    return pl.pallas_call(
        functools.partial(_kpass_body, n_blocks=n_blocks),
        grid_spec=grid_spec,
        out_shape=[jax.ShapeDtypeStruct((nb, H_ATT, n_pages * PAGE_ROWS), F32),
                   jax.ShapeDtypeStruct((nb, H_ATT, LANES), I32)],
        compiler_params=_cparams(("arbitrary", "arbitrary")),
        name="moba_decode_kpass",
    )(page_table, q3, *([cache_k2] * KP_PAGES))


VP_HEADS = 4
VP_PAGES = VP_HEADS * MOBA_TOPK * PAGES_PER_BLOCK


def _vpass_body(pt_ref, idx_ref, lg_ref, q_ref, kn_ref, vn_ref, bkt_ref, tblt_ref, *refs, n_blocks):
    vpages = refs[:VP_PAGES]
    o_ref = refs[VP_PAGES]
    b = pl.program_id(0)
    hf = pl.program_id(1)
    scale = HEAD_DIM ** -0.5
    lane = lax.broadcasted_iota(I32, (1, BLOCK_ROWS), 1)
    bkt = bkt_ref[...]
    tblt = tblt_ref[0]
    q = q_ref[0, 0]
    lg_self = jnp.sum(q * kn_ref[0, 0], -1, keepdims=True) * scale + tblt[:, 0:1]

    for hh in range(VP_HEADS):
        h = hf * VP_HEADS + hh
        mine = (lane & (H_ATT - 1)) == h
        trow = tblt[hh:hh + 1]
        bias_last = jnp.zeros((1, BLOCK_ROWS), F32)
        for bb in range(N_BUCKETS):
            bias_last = jnp.where(bkt == bb, trow[:, bb:bb + 1], bias_last)
        bias_far = trow[:, N_BUCKETS - 1:N_BUCKETS]
        xs = []
        ls = lg_self[hh:hh + 1]
        mx = ls
        for j in range(MOBA_TOPK):
            blk = idx_ref[b, h, j]
            start = pl.multiple_of(blk * BLOCK_ROWS, BLOCK_ROWS)
            raw = lg_ref[0, 0, hh:hh + 1, pl.ds(start, BLOCK_ROWS)]
            x = jnp.where(mine, raw * scale + jnp.where(blk == n_blocks - 1, bias_last, bias_far), NEG)
            xs.append(x)
            mx = jnp.maximum(mx, jnp.max(x, -1, keepdims=True))
        p_self = jnp.exp(ls - mx)
        den = p_self
        acc = p_self * vn_ref[0, 0, hh:hh + 1]
        for j in range(MOBA_TOPK):
            pj = jnp.exp(xs[j] - mx)
            den = den + jnp.sum(pj, -1, keepdims=True)
            for pg in range(PAGES_PER_BLOCK):
                vp = vpages[(hh * MOBA_TOPK + j) * PAGES_PER_BLOCK + pg][0]
                acc = acc + _bdot(pj[:, pg * PAGE_ROWS:(pg + 1) * PAGE_ROWS], vp)
        o_ref[0, 0, hh:hh + 1, :] = acc / den


def _vpass(page_table, idx, logits4, q4, kn4, vn4, bkt_last, tbl_t3, cache_v2):
    nb, n_pages = page_table.shape
    n_blocks = n_pages // PAGES_PER_BLOCK
    n_half = H_ATT // VP_HEADS
    r4 = lambda s: pl.BlockSpec((1, 1) + s, lambda b, hf, pt, ix: (b, hf) + (0,) * len(s))

    def page_spec(hh, j, pg):
        return pl.BlockSpec((1, PAGE_ROWS, HEAD_DIM),
                            lambda b, hf, pt, ix: (pt[b, ix[b, hf * VP_HEADS + hh, j] * PAGES_PER_BLOCK + pg], 0, 0))

    grid_spec = pltpu.PrefetchScalarGridSpec(
        num_scalar_prefetch=2,
        grid=(nb, n_half),
        in_specs=[r4((VP_HEADS, n_pages * PAGE_ROWS)), r4((VP_HEADS, HEAD_DIM)), r4((VP_HEADS, HEAD_DIM)),
                  r4((VP_HEADS, HEAD_DIM)), _whole(),
                  pl.BlockSpec((1, VP_HEADS, N_BUCKETS), lambda b, hf, pt, ix: (hf, 0, 0))]
        + [page_spec(hh, j, pg) for hh in range(VP_HEADS) for j in range(MOBA_TOPK) for pg in range(PAGES_PER_BLOCK)],
        out_specs=r4((VP_HEADS, HEAD_DIM)),
    )
    return pl.pallas_call(
        functools.partial(_vpass_body, n_blocks=n_blocks),
        grid_spec=grid_spec,
        out_shape=jax.ShapeDtypeStruct((nb, n_half, VP_HEADS, HEAD_DIM), F32),
        compiler_params=_cparams(("arbitrary", "arbitrary")),
        name="moba_decode_vpass",
    )(page_table, idx, logits4, q4, kn4, vn4, bkt_last, tbl_t3, *([cache_v2] * VP_PAGES))


def _pair_layout_ab(w_a, w_b):
    d = w_a.shape[0]
    out = jnp.zeros((d, N_PAIR, LANES), w_a.dtype)
    out = out.at[:, :, 0:2].set(w_a.reshape(d, N_PAIR, 2))
    out = out.at[:, :, 2:4].set(w_b.reshape(d, N_PAIR, 2))
    return out.reshape(d, N_PAIR * LANES)


def kernel(x_prompt, x_sample, p_prompt, p_sample, cache_k, cache_v, state_dn, state_conv, page_table, w_in, conv_w, a_log, dt_bias, g_dn, w_o, rel_bias, g_pre_mix, g_post_mix, g_pre_mlp, g_post_mlp, w_up, w_down, g_ple, w_pg, w_ple):
    batch, seq, d = x_prompt.shape
    nb = x_sample.shape[0]
    depth = w_in.shape[0]
    assert depth == 1 and x_sample.shape[1] == 1
    assert seq % MOBA_KV == 0 and seq // MOBA_BLOCK <= LANES - HEAD_DIM
    assert page_table.shape[1] * PAGE_SIZE // MOBA_BLOCK >= MOBA_TOPK
    l = 0
    row = lambda a: a.reshape(1, -1)

    wi = w_in[l]
    o_gate = D_CONV + D_DN
    o_att = o_gate + 2 * H_DN
    w_main = jnp.concatenate([wi[:, :o_gate], wi[:, o_att:]], axis=1).astype(BF16)
    w_ab = _pair_layout_ab(wi[:, o_gate:o_gate + H_DN], wi[:, o_gate + H_DN:o_att]).astype(BF16)
    hpar = jnp.zeros((N_PAIR, 8, LANES), F32)
    hpar = hpar.at[:, 0, 0:2].set(a_log[l].reshape(N_PAIR, 2)).at[:, 1, 0:2].set(dt_bias[l].reshape(N_PAIR, 2))
    gdn2 = jnp.tile(g_dn[l], 2).reshape(1, LANES)
    wo, wup, wdn, wpg, wple = (w.astype(BF16) for w in (w_o[l], w_up[l], w_down[l], w_pg[l], w_ple[l]))
    tail_w = (wo, row(g_post_mix[l]), row(g_pre_mlp[l]), wup, wdn, row(g_post_mlp[l]), row(g_ple[l]), wpg, wple)

    m = batch * seq
    xp = x_prompt.reshape(m, d)
    qkv_raw, gate, ab, q_att, k_att, v_att, kmeans = _inproj(xp, row(g_pre_mix[l]), w_main, w_ab, 512, True)
    y_dn, sfin = _dn_prompt(qkv_raw, gate, ab, conv_w[l], hpar, gdn2, batch)
    nblk = seq // MOBA_BLOCK
    km_pad = jnp.zeros((batch, LANES, D_ATT), F32).at[:, :nblk].set(kmeans.reshape(batch, nblk, D_ATT))
    bsd = lambda a: a.reshape(batch, seq, D_ATT)
    o_at = _moba_prompt(bsd(q_att), bsd(k_att), bsd(v_att).transpose(0, 2, 1).astype(BF16), km_pad,
                        rel_bias).reshape(m, D_ATT)
    y_prompt = _tail(xp, y_dn, o_at, p_prompt[l].reshape(m, -1), *tail_w, tile=256).reshape(batch, seq, d)

    k_prompt = k_att.reshape(1, batch, seq, H_ATT, HEAD_DIM)
    v_prompt = v_att.reshape(1, batch, seq, H_ATT, HEAD_DIM)
    sf = sfin.reshape(batch, N_PAIR, 2, HEAD_DIM, 2, HEAD_DIM)
    dn_prompt = jnp.stack([sf[:, :, 0, :, 0, :], sf[:, :, 1, :, 1, :]], axis=2).reshape(1, batch, H_DN, HEAD_DIM, HEAD_DIM)
    conv_prompt = qkv_raw.reshape(batch, seq, D_CONV)[:, seq - (CONV_W - 1):, :].reshape(1, batch, CONV_W - 1, D_CONV)

    xs = x_sample.reshape(nb, d)
    qkv_s, gate_s, ab_s, q_s, k_s, v_s = _inproj(xs, row(g_pre_mix[l]), w_main, w_ab, nb, False)
    ab4 = ab_s.reshape(nb, N_PAIR, LANES)
    a3 = ab4[:, :, 0:2].reshape(nb, H_DN, 1)
    b3 = ab4[:, :, 2:4].reshape(nb, H_DN, 1)
    y3, cnew, snew = _dn_step(
        qkv_s.reshape(nb, 3 * H_DN, HEAD_DIM), state_conv[l].reshape(nb, CONV_W - 1, 3 * H_DN, HEAD_DIM),
        conv_w[l].reshape(CONV_W, 3 * H_DN, HEAD_DIM), a3, b3, a_log[l].reshape(H_DN, 1), dt_bias[l].reshape(H_DN, 1),
        gate_s.reshape(nb, H_DN, HEAD_DIM), row(g_dn[l]), state_dn[l])
    n_pages = page_table.shape[1]
    n_pool = cache_k.shape[1]
    ck2 = cache_k[l].reshape(n_pool, PAGE_ROWS, HEAD_DIM)
    cv2 = cache_v[l].reshape(n_pool, PAGE_ROWS, HEAD_DIM)
    logits, idx = _kpass(page_table, q_s.reshape(nb, H_ATT, HEAD_DIM), ck2)
    n_half = H_ATT // VP_HEADS
    split = lambda a: a.reshape(nb, n_half, VP_HEADS, a.shape[-1] // H_ATT if a.ndim == 2 else a.shape[-1])
    bkt_last = jnp.asarray(np.repeat(_t5_bucket_np(MOBA_BLOCK - np.arange(MOBA_BLOCK)), H_ATT).reshape(1, BLOCK_ROWS))
    o_s = _vpass(page_table, idx[:, :, :MOBA_TOPK], split(logits), split(q_s), split(k_s), split(v_s), bkt_last,
                 rel_bias.T.reshape(n_half, VP_HEADS, N_BUCKETS), cv2)
    y_sample = _tail(xs, y3.reshape(nb, D_DN), o_s.reshape(nb, D_ATT), p_sample[l].reshape(nb, -1), *tail_w,
                     tile=nb).reshape(nb, 1, d)

    k_sample = k_s.reshape(1, nb, 1, H_ATT, HEAD_DIM)
    v_sample = v_s.reshape(1, nb, 1, H_ATT, HEAD_DIM)
    dn_sample = snew.reshape(1, nb, H_DN, HEAD_DIM, HEAD_DIM)
    conv_sample = cnew.reshape(1, nb, CONV_W - 1, D_CONV)
    return (y_prompt, y_sample, k_prompt, v_prompt, dn_prompt, conv_prompt,
            k_sample, v_sample, dn_sample, conv_sample)
```

```python
import functools
import math

import numpy as np
import jax
import jax.numpy as jnp
from jax import lax
from jax.experimental import pallas as pl
from jax.experimental.pallas import tpu as pltpu

F32 = jnp.float32
BF16 = jnp.bfloat16
I32 = jnp.int32

HEAD_DIM = 64
H_DN = 8
H_ATT = 8
D_DN = H_DN * HEAD_DIM
D_ATT = H_ATT * HEAD_DIM
D_CONV = 3 * D_DN
CONV_W = 4
MOBA_BLOCK = 256
MOBA_TOPK = 3
N_BUCKETS = 32
MAX_DISTANCE = 128
PAGE_SIZE = 128
EPS = 1e-6
NEG = -1e30
LANES = 128
N_PAIR = H_DN // 2
VMEM_LIMIT = 56 * 1024 * 1024

NN = (((1,), (0,)), ((), ()))
NT = (((1,), (1,)), ((), ()))


def _dot(a, b, dims=NN):
    return lax.dot_general(a, b, dims, preferred_element_type=F32)


def _bdot(a, b, dims=NN):
    return _dot(a.astype(BF16), b.astype(BF16), dims)


def _split(a):
    hi = a.astype(BF16)
    lo = (a - hi.astype(F32)).astype(BF16)
    return hi, lo


def _dot3(a, b, dims=NN):
    ah, al = _split(a)
    bh, bl = _split(b)
    return _dot(ah, bh, dims) + (_dot(ah, bl, dims) + _dot(al, bh, dims))


def _mm(a, b, dims=NN, passes=1):
    return _bdot(a, b, dims) if passes == 1 else _dot3(a, b, dims)


def _dot_exact_lhs(a_bf16, b, dims=NN):
    b1 = b.astype(BF16)
    r1 = b - b1.astype(F32)
    b2 = r1.astype(BF16)
    b3 = (r1 - b2.astype(F32)).astype(BF16)
    return _dot(a_bf16, b1, dims) + (_dot(a_bf16, b2, dims) + _dot(a_bf16, b3, dims))


def _rms(x, g):
    return x * lax.rsqrt(jnp.mean(x * x, -1, keepdims=True) + EPS) * g


def _sigmoid(x):
    return 1.0 / (1.0 + jnp.exp(-x))


def _silu(x):
    return x * _sigmoid(x)


def _softplus(z):
    return jnp.maximum(z, 0.0) + jnp.log1p(jnp.exp(-jnp.abs(z)))


def _t5_bucket_np(n):
    n = np.asarray(n)
    max_exact = N_BUCKETS // 2
    nf = np.maximum(n, 1).astype(np.float32)
    large = max_exact + (np.log(nf / np.float32(max_exact)) / np.float32(math.log(MAX_DISTANCE / max_exact))
                         * np.float32(N_BUCKETS - max_exact)).astype(np.int32)
    large = np.minimum(large, N_BUCKETS - 1)
    return np.where(n < max_exact, n, large).astype(np.int32)


def _cparams(sem):
    return pltpu.CompilerParams(dimension_semantics=sem, vmem_limit_bytes=VMEM_LIMIT)


def _whole():
    return pl.BlockSpec(memory_space=pltpu.VMEM)


def _inproj_body(x_ref, g_ref, wm_ref, wab_ref, *refs, tile, prompt):
    if prompt:
        wkvt_ref, qkv_ref, gate_ref, ab_ref, q_ref, k_ref, kt_ref, vt_ref, km_ref = refs
    else:
        qkv_ref, gate_ref, ab_ref, q_ref, k_ref, v_ref = refs
    x = x_ref[...]
    xn = _rms(x, g_ref[...]).astype(BF16)
    qkv_ref[...] = _dot(xn, wm_ref[:, 0:D_CONV])
    o = D_CONV
    gate_ref[...] = _dot(xn, wm_ref[:, o:o + D_DN])
    o += D_DN
    q_ref[...] = _dot(xn, wm_ref[:, o:o + D_ATT])
    o += D_ATT
    k = _dot(xn, wm_ref[:, o:o + D_ATT])
    k_ref[...] = k
    o += D_ATT
    ab_ref[...] = _dot(xn, wab_ref[...])
    if prompt:
        kt_ref[0] = _dot(wkvt_ref[0:D_ATT, :], xn, NT)
        vt_ref[0] = _dot(wkvt_ref[D_ATT:, :], xn, NT)
        for i in range(tile // MOBA_BLOCK):
            km_ref[i] = jnp.mean(k[i * MOBA_BLOCK:(i + 1) * MOBA_BLOCK], axis=0, keepdims=True)
    else:
        v_ref[...] = _dot(xn, wm_ref[:, o:o + D_ATT])


def _inproj(x, g, w_main, w_ab, tile, w_kvt=None, seq=None):
    m, d = x.shape
    prompt = w_kvt is not None
    row = lambda w: pl.BlockSpec((tile, w), lambda i: (i, 0))
    out_shape = [jax.ShapeDtypeStruct((m, D_CONV), F32), jax.ShapeDtypeStruct((m, D_DN), F32),
                 jax.ShapeDtypeStruct((m, N_PAIR * LANES), F32), jax.ShapeDtypeStruct((m, D_ATT), F32),
                 jax.ShapeDtypeStruct((m, D_ATT), F32)]
    out_specs = [row(D_CONV), row(D_DN), row(N_PAIR * LANES), row(D_ATT), row(D_ATT)]
    in_specs = [row(d), _whole(), _whole(), _whole()]
    args = [x, g, w_main, w_ab]
    if prompt:
        tps = seq // tile
        t_spec = pl.BlockSpec((1, D_ATT, tile), lambda i: (i // tps, 0, i % tps))
        out_shape += [jax.ShapeDtypeStruct((m // seq, D_ATT, seq), F32)] * 2
        out_specs += [t_spec, t_spec]
        out_shape.append(jax.ShapeDtypeStruct((m // MOBA_BLOCK, 1, D_ATT), F32))
        out_specs.append(pl.BlockSpec((tile // MOBA_BLOCK, 1, D_ATT), lambda i: (i, 0, 0)))
        in_specs.append(_whole())
        args.append(w_kvt)
    else:
        out_shape.append(jax.ShapeDtypeStruct((m, D_ATT), F32))
        out_specs.append(row(D_ATT))
    return pl.pallas_call(
        functools.partial(_inproj_body, tile=tile, prompt=prompt),
        grid=(m // tile,),
        in_specs=in_specs,
        out_specs=out_specs,
        out_shape=out_shape,
        compiler_params=_cparams(("arbitrary",)),
        name="inproj_prompt" if prompt else "inproj",
    )(*args)


DN_CHUNK = 256
DN_BASE = 2


def _pair_sum(x, lo):
    s0 = jnp.sum(jnp.where(lo, x, 0.0), -1, keepdims=True)
    s1 = jnp.sum(jnp.where(lo, 0.0, x), -1, keepdims=True)
    return jnp.where(lo, s0, s1)


def _dn_body(q_ref, k_ref, v_ref, gate_ref, ab_ref, cwq_ref, cwk_ref, cwv_ref, hpar_ref, gdn_ref,
             y_ref, sfin_ref, bq, bk, bv, s_ref, *, passes_inv, passes_mm):
    C = DN_CHUNK
    t = pl.program_id(2)
    nt = pl.num_programs(2)

    @pl.when(t == 0)
    def _():
        s_ref[...] = jnp.zeros_like(s_ref)
        for b in (bq, bk, bv):
            b[0:8, :] = jnp.zeros((8, LANES), F32)

    lane = lax.broadcasted_iota(I32, (C, LANES), 1)
    lo = lane < HEAD_DIM

    def conv(x_ref, buf, w_ref):
        x = x_ref[...]
        buf[8:8 + C, :] = x
        w = w_ref[...]
        y = (x * w[3:4] + buf[7:7 + C, :] * w[2:3] + buf[6:6 + C, :] * w[1:2] + buf[5:5 + C, :] * w[0:1])
        buf[0:8, :] = x[C - 8:C, :]
        return _silu(y)

    q = conv(q_ref, bq, cwq_ref)
    k = conv(k_ref, bk, cwk_ref)
    v = conv(v_ref, bv, cwv_ref)
    q = q * lax.rsqrt(_pair_sum(q * q, lo) + EPS) * (HEAD_DIM ** -0.5)
    k = k * lax.rsqrt(_pair_sum(k * k, lo) + EPS)

    ab = ab_ref[...]
    hpar = hpar_ref[0]
    g_all = -jnp.exp(hpar[0:1]) * _softplus(ab + hpar[1:2])
    beta_all = _sigmoid(ab)

    HB = C // 2
    row = lax.broadcasted_iota(I32, (C, C), 0)
    col = lax.broadcasted_iota(I32, (C, C), 1)
    causal = row >= col
    strict = row > col
    rh = lax.broadcasted_iota(I32, (HB, HB), 0)
    ch = lax.broadcasted_iota(I32, (HB, HB), 1)
    ltri = (rh >= ch).astype(BF16)
    eye = (rh == ch).astype(F32)
    gam_top = _dot_exact_lhs(ltri, g_all[0:HB])
    gam_all = jnp.concatenate([gam_top, _dot_exact_lhs(ltri, g_all[HB:]) + gam_top[HB - 1:HB]], axis=0)
    gam_t = gam_all.T

    def blk_id(idx, size):
        return lax.shift_right_logical(idx, size.bit_length() - 1)

    def inv_halves(blocks):
        ps = [eye - jnp.where(blk_id(rh, DN_BASE) == blk_id(ch, DN_BASE), a, 0.0) for a in blocks]
        size = DN_BASE
        while size < HB:
            pick = jnp.logical_and(blk_id(rh, 2 * size) == blk_id(ch, 2 * size),
                                   blk_id(rh, size) != blk_id(ch, size))
            ws = [_mm(jnp.where(pick, a, 0.0), p, NN, passes_inv) for a, p in zip(blocks, ps)]
            ps = [p - _mm(p, w, NN, passes_inv) for p, w in zip(ps, ws)]
            size *= 2
        return ps

    def head_pre(i):
        sel = lo if i == 0 else jnp.logical_not(lo)
        decay = jnp.where(causal, jnp.exp(jnp.where(causal, gam_all[:, i:i + 1] - gam_t[i:i + 1, :], 0.0)), 0.0)
        kk = _mm(jnp.where(sel, k, 0.0), k, NT, passes_mm)
        a = jnp.where(strict, beta_all[:, 2 + i:3 + i] * kk * decay, 0.0)
        qk = jnp.where(causal, _mm(jnp.where(sel, q, 0.0), k, NT, passes_mm) * decay, 0.0)
        return a, qk

    a0, qk0 = head_pre(0)
    a1, qk1 = head_pre(1)
    p00, p01, p10, p11 = inv_halves([a0[0:HB, 0:HB], a0[HB:, HB:], a1[0:HB, 0:HB], a1[HB:, HB:]])
    y0 = _mm(a0[HB:, 0:HB], p00, NN, passes_inv)
    y1 = _mm(a1[HB:, 0:HB], p10, NN, passes_inv)
    t0, b0 = p00, jnp.concatenate([-_mm(p01, y0, NN, passes_inv), p01], axis=1)
    t1, b1 = p10, jnp.concatenate([-_mm(p11, y1, NN, passes_inv), p11], axis=1)

    gam_p = jnp.where(lo, gam_all[:, 0:1], gam_all[:, 1:2])
    beta_p = jnp.where(lo, beta_all[:, 2:3], beta_all[:, 3:4])
    lo_row = lax.broadcasted_iota(I32, (1, LANES), 1) < HEAD_DIM
    glast_p = jnp.where(lo_row, gam_all[C - 1:C, 0:1], gam_all[C - 1:C, 1:2])
    egam = jnp.exp(gam_p)
    xuw = jnp.concatenate([v * beta_p, k * beta_p * egam], axis=1)

    def apply_rows(top, bot, x):
        return jnp.concatenate([_mm(top, x[0:HB], NN, passes_mm), _mm(bot, x, NN, passes_mm)], axis=0)

    uw0 = apply_rows(t0, b0, xuw)
    uw1 = apply_rows(t1, b1, xuw)
    u = jnp.where(lo, uw0[:, 0:LANES], uw1[:, 0:LANES])
    w = jnp.where(lo, uw0[:, LANES:], uw1[:, LANES:])

    s = s_ref[...]
    v_new = u - _mm(w, s, NN, passes_mm)
    o = _mm(q * egam, s, NN, passes_mm) + jnp.where(lo, apply_rows(qk0[0:HB, 0:HB], qk0[HB:], v_new),
                                                   apply_rows(qk1[0:HB, 0:HB], qk1[HB:], v_new))
    kd = k * jnp.exp(glast_p - gam_p)
    r2 = lax.broadcasted_iota(I32, (LANES, LANES), 0)
    c2 = lax.broadcasted_iota(I32, (LANES, LANES), 1)
    same_head = (r2 < HEAD_DIM) == (c2 < HEAD_DIM)
    s_new = s * jnp.exp(glast_p) + jnp.where(same_head, _mm(kd.T, v_new, NN, passes_mm), 0.0)
    s_ref[...] = s_new

    ms = _pair_sum(o * o, lo) * (1.0 / HEAD_DIM)
    y_ref[...] = o * lax.rsqrt(ms + EPS) * gdn_ref[...] * _silu(gate_ref[...])

    @pl.when(t == nt - 1)
    def _():
        sfin_ref[0, 0] = s_new


def _dn_prompt(qkv_raw, gate, ab, conv_w, hpar, gdn2, batch, passes_inv=1, passes_mm=1):
    m = qkv_raw.shape[0]
    t_len = m // batch
    nt = t_len // DN_CHUNK
    C = DN_CHUNK
    blk = lambda off: pl.BlockSpec((C, LANES), lambda b, hp, t: (b * nt + t, off + hp))
    cw = lambda off: pl.BlockSpec((CONV_W, LANES), lambda b, hp, t: (0, off + hp))
    y, sfin = pl.pallas_call(
        functools.partial(_dn_body, passes_inv=passes_inv, passes_mm=passes_mm),
        grid=(batch, N_PAIR, nt),
        in_specs=[blk(0), blk(N_PAIR), blk(2 * N_PAIR), blk(0), blk(0),
                  cw(0), cw(N_PAIR), cw(2 * N_PAIR),
                  pl.BlockSpec((1, 8, LANES), lambda b, hp, t: (hp, 0, 0)),
                  pl.BlockSpec((1, LANES), lambda b, hp, t: (0, 0))],
        out_specs=[blk(0), pl.BlockSpec((1, 1, LANES, LANES), lambda b, hp, t: (b, hp, 0, 0))],
        out_shape=[jax.ShapeDtypeStruct((m, D_DN), F32), jax.ShapeDtypeStruct((batch, N_PAIR, LANES, LANES), F32)],
        scratch_shapes=[pltpu.VMEM((8 + C, LANES), F32)] * 3 + [pltpu.VMEM((LANES, LANES), F32)],
        compiler_params=_cparams(("arbitrary", "arbitrary", "arbitrary")),
        name="deltanet_prompt",
    )(qkv_raw, qkv_raw, qkv_raw, gate, ab, conv_w, conv_w, conv_w, hpar, gdn2)
    return y, sfin


MOBA_G = 4
MOBA_KV = MOBA_G * MOBA_BLOCK
MOBA_CHUNK = 128
MOBA_ONES = 16
LOG2E = math.log2(math.e)
_BUCKET_START = [int(np.argmax(_t5_bucket_np(np.arange(4 * MAX_DISTANCE)) >= b)) for b in range(1, N_BUCKETS)]


def _moba_body(qi_tab, t_tab, kind_tab, q_ref, k_ref, vt_ref, km_ref, tbl_ref, o_ref,
               bias_sc, qaug_sc, m_sc, acc_sc, st_sc, *, nblk_pad, nbatch):
    B, KV = MOBA_BLOCK, MOBA_KV
    hp = pl.program_id(0)
    s_id = pl.program_id(1)
    qi = qi_tab[s_id]
    t = t_tab[s_id]
    kind = kind_tab[s_id]
    t_own = qi // MOBA_G
    first = t == t_own
    last = jnp.logical_or(t == t_own - 1, t_own == 0)

    @pl.when(s_id == 0)
    def _():
        c = lax.broadcasted_iota(I32, (B, B), 0)
        r = lax.broadcasted_iota(I32, (B, B), 1)
        for i in range(2):
            h = 2 * hp + i
            far = tbl_ref[N_BUCKETS - 1, h]
            for j in range(2):
                dist = r + j * B - c
                acc = jnp.full((B, B), tbl_ref[0, h], F32)
                for b in range(1, N_BUCKETS):
                    acc = jnp.where(dist >= _BUCKET_START[b - 1], tbl_ref[b, h], acc)
                bias_sc[i, j] = jnp.where(dist < 0, NEG, (acc - far) * LOG2E)
            bias_sc[i, 2] = jnp.zeros((B, B), F32)

    lane = lax.broadcasted_iota(I32, (B, LANES), 1)
    lo = lane < HEAD_DIM

    @pl.when(first)
    def _():
        nrow = lax.broadcasted_iota(I32, (nblk_pad, B), 0)
        for bb in range(nbatch):
            q = q_ref[bb]
            km = km_ref[bb]
            for i in range(2):
                head = lo if i == 0 else jnp.logical_not(lo)
                st = _dot3(km, jnp.where(head, q, 0.0), NT)[0:nblk_pad]
                cur = jnp.where(nrow < qi, st, NEG)
                sel = nrow == qi
                for _ in range(MOBA_TOPK):
                    mx = jnp.max(cur, 0, keepdims=True)
                    pick = nrow == jnp.min(jnp.where(cur == mx, nrow, nblk_pad), 0, keepdims=True)
                    sel = jnp.logical_or(sel, jnp.logical_and(pick, mx > 0.5 * NEG))
                    cur = jnp.where(pick, NEG, cur)
                xt = jnp.where(sel, 0.0, NEG)
                off = HEAD_DIM if i == 0 else 0
                parts = [xt, jnp.zeros((LANES - off - nblk_pad, B), F32)]
                if off:
                    parts = [jnp.zeros((off, B), F32)] + parts
                x = jnp.concatenate(parts, axis=0).T
                qaug_sc[bb, i] = jnp.where(head, q * (HEAD_DIM ** -0.5 * LOG2E), x).astype(BF16)
                m_sc[bb, i] = jnp.full((1, B), NEG, F32)
                acc_sc[bb, i] = jnp.zeros((HEAD_DIM + MOBA_ONES, B), F32)

    CH = MOBA_CHUNK
    n_ck = KV // CH
    lane1 = lax.broadcasted_iota(I32, (1, LANES), 1)
    klo = lax.broadcasted_iota(I32, (CH, LANES), 1) < HEAD_DIM
    probs = [(bb, i) for bb in range(nbatch) for i in range(2)]

    def step(with_bias):
        ones = jnp.ones((MOBA_ONES, CH), BF16)

        def p1_chunk(bb, i, ck, mx):
            kb = k_ref[bb, pl.ds(ck * CH, CH), :].astype(BF16)
            blk_lane = t * MOBA_G + (ck * CH) // MOBA_BLOCK + (HEAD_DIM if i == 0 else 0)
            yrow = jnp.where(lane1 == blk_lane, 1.0, 0.0).astype(BF16)
            kaug = jnp.where(klo, kb, yrow) if i == 0 else jnp.where(klo, yrow, kb)
            s = _dot(kaug, qaug_sc[bb, i], NT)
            if with_bias:
                dq = qi - (t * MOBA_G + (ck * CH) // MOBA_BLOCK)
                which = jnp.where(dq == 0, 0, jnp.where(dq == 1, 1, 2))
                s = s + bias_sc[i, which, pl.ds((ck * CH) % MOBA_BLOCK, CH), :]
            s3 = s.reshape(CH // 8, 8, B)
            st_sc[bb, i, pl.ds(ck * (CH // 8), CH // 8)] = s3
            cm = jnp.max(s3, axis=0)
            return cm if mx is None else jnp.maximum(mx, cm)

        def p1_done(bb, i, mx):
            m_old = m_sc[bb, i]
            m_new = jnp.maximum(m_old, jnp.max(mx, 0, keepdims=True))
            m_sc[bb, i] = m_new
            return jnp.exp2(m_old - m_new), jnp.broadcast_to(m_new, (8, B))

        def p2_chunk(bb, i, ck, m8, pv):
            p3 = jnp.exp2(st_sc[bb, i, pl.ds(ck * (CH // 8), CH // 8)] - m8[None])
            p = p3.reshape(CH, B).astype(BF16)
            vtc = vt_ref[bb, i * HEAD_DIM:(i + 1) * HEAD_DIM, pl.ds(ck * CH, CH)].astype(BF16)
            vaug = jnp.concatenate([vtc, ones], axis=0)
            d = _dot(vaug, p)
            return d if pv is None else pv + d

        prev = None
        for cur in probs + [None]:
            mx, pv = None, None
            for ck in range(n_ck):
                if cur is not None:
                    mx = p1_chunk(*cur, ck, mx)
                if prev is not None:
                    pv = p2_chunk(*prev, ck, prev_m8, pv)
            if prev is not None:
                acc_sc[prev] = acc_sc[prev] * prev_alpha + pv
            if cur is not None:
                prev_alpha, prev_m8 = p1_done(*cur, mx)
            prev = cur

    @pl.when(kind > 0)
    def _():
        step(True)

    @pl.when(kind == 0)
    def _():
        step(False)

    @pl.when(last)
    def _():
        for bb in range(nbatch):
            a0 = acc_sc[bb, 0]
            a1 = acc_sc[bb, 1]
            ot = jnp.concatenate([a0[0:HEAD_DIM] / a0[HEAD_DIM:HEAD_DIM + 1],
                                  a1[0:HEAD_DIM] / a1[HEAD_DIM:HEAD_DIM + 1]], axis=0)
            o_ref[bb] = ot.T


def _moba_prompt(q, k, vt, km_pad, rel_bias):
    batch, seq, _ = q.shape
    nt = seq // MOBA_BLOCK
    nblk_pad = -(-nt // 8) * 8
    qi_l, t_l, kind_l = [], [], []
    for qi in range(nt):
        t_own = qi // MOBA_G
        for t in [t_own] + list(range(t_own)):
            dq = qi - t * MOBA_G
            qi_l.append(qi)
            t_l.append(t)
            kind_l.append(1 if dq <= MOBA_G else 0)
    tabs = [jnp.asarray(np.array(a, np.int32)) for a in (qi_l, t_l, kind_l)]
    B, KV = MOBA_BLOCK, MOBA_KV
    qspec = pl.BlockSpec((batch, B, LANES), lambda hp, s, qt, tt, kt: (0, qt[s], hp))
    grid_spec = pltpu.PrefetchScalarGridSpec(
        num_scalar_prefetch=3,
        grid=(N_PAIR, len(qi_l)),
        in_specs=[qspec,
                  pl.BlockSpec((batch, KV, LANES), lambda hp, s, qt, tt, kt: (0, tt[s], hp)),
                  pl.BlockSpec((batch, LANES, KV), lambda hp, s, qt, tt, kt: (0, hp, tt[s])),
                  pl.BlockSpec((batch, LANES, LANES), lambda hp, s, qt, tt, kt: (0, 0, hp)),
                  pl.BlockSpec(memory_space=pltpu.SMEM)],
        out_specs=qspec,
        scratch_shapes=[pltpu.VMEM((2, 3, B, B), F32), pltpu.VMEM((batch, 2, B, LANES), BF16),
                        pltpu.VMEM((batch, 2, 1, B), F32), pltpu.VMEM((batch, 2, HEAD_DIM + MOBA_ONES, B), F32),
                        pltpu.VMEM((batch, 2, KV // 8, 8, B), F32)],
    )
    return pl.pallas_call(
        functools.partial(_moba_body, nblk_pad=nblk_pad, nbatch=batch),
        grid_spec=grid_spec,
        out_shape=jax.ShapeDtypeStruct((batch, seq, D_ATT), F32),
        compiler_params=_cparams(("arbitrary", "arbitrary")),
        name="moba_prompt",
    )(*tabs, q, k, vt, km_pad, rel_bias)


FF_CHUNK = 1024


def _tail_body(x_ref, ydn_ref, oat_ref, p_ref, wo_ref, gpm_ref, gpre_ref, wup_ref, wdn_ref, gpost_ref,
               gple_ref, wpg_ref, wple_ref, o_ref):
    t = _dot(ydn_ref[...].astype(BF16), wo_ref[0:D_DN, :]) + _dot(oat_ref[...].astype(BF16), wo_ref[D_DN:, :])
    h = x_ref[...] + _rms(t, gpm_ref[...])
    hn = _rms(h, gpre_ref[...]).astype(BF16)
    d_ff = wup_ref.shape[1]
    acc = jnp.zeros_like(h)
    for c in range(d_ff // FF_CHUNK):
        u = jnp.maximum(_dot(hn, wup_ref[:, c * FF_CHUNK:(c + 1) * FF_CHUNK]), 0.0)
        acc = acc + _dot((u * u).astype(BF16), wdn_ref[c * FF_CHUNK:(c + 1) * FF_CHUNK, :])
    h = h + _rms(acc, gpost_ref[...])
    gate = _sigmoid(_dot(_rms(h, gple_ref[...]).astype(BF16), wpg_ref[...]))
    o_ref[...] = h + _dot(p_ref[...].astype(BF16), wple_ref[...]) * gate


def _tail(x, ydn, oat, p, wo, gpm, gpre, wup, wdn, gpost, gple, wpg, wple, tile):
    m, d = x.shape
    row = lambda w: pl.BlockSpec((tile, w), lambda i: (i, 0))
    return pl.pallas_call(
        _tail_body,
        grid=(m // tile,),
        in_specs=[row(d), row(D_DN), row(D_ATT), row(p.shape[1])] + [_whole()] * 9,
        out_specs=row(d),
        out_shape=jax.ShapeDtypeStruct((m, d), F32),
        compiler_params=_cparams(("arbitrary",)),
        name="block_tail",
    )(x, ydn, oat, p, wo, gpm, gpre, wup, wdn, gpost, gple, wpg, wple)


def _dn_step_body(x_ref, cst_ref, cw_ref, a_ref, b_ref, alog_ref, dtb_ref, gate_ref, gdn_ref, s_ref,
                  y_ref, cnew_ref, snew_ref):
    x = x_ref[0]
    cst = cst_ref[0]
    cw = cw_ref[...]
    y = cst[0] * cw[0] + cst[1] * cw[1] + cst[2] * cw[2] + x * cw[3]
    y = _silu(y)
    cnew_ref[0, 0] = cst[1]
    cnew_ref[0, 1] = cst[2]
    cnew_ref[0, 2] = x
    nrm = lax.rsqrt(jnp.sum(y * y, -1, keepdims=True) + EPS)
    q = y[0:H_DN] * nrm[0:H_DN] * (HEAD_DIM ** -0.5)
    k = y[H_DN:2 * H_DN] * nrm[H_DN:2 * H_DN]
    v = y[2 * H_DN:3 * H_DN]
    g = -jnp.exp(alog_ref[...]) * _softplus(a_ref[0] + dtb_ref[...])
    beta = _sigmoid(b_ref[0])
    r = lax.broadcasted_iota(I32, (HEAD_DIM, HEAD_DIM), 0)
    c = lax.broadcasted_iota(I32, (HEAD_DIM, HEAD_DIM), 1)
    eye = r == c

    def to_col(vrow):
        return jnp.sum(jnp.where(eye, jnp.broadcast_to(vrow, (HEAD_DIM, HEAD_DIM)), 0.0), -1, keepdims=True)

    gdn = gdn_ref[...]
    for h in range(H_DN):
        s = s_ref[0, h] * jnp.exp(g[h:h + 1])
        kcol = to_col(k[h:h + 1])
        err = v[h:h + 1] - jnp.sum(s * kcol, 0, keepdims=True)
        s = s + (kcol * beta[h:h + 1]) * err
        snew_ref[0, h] = s
        o = jnp.sum(s * to_col(q[h:h + 1]), 0, keepdims=True)
        y_ref[0, h:h + 1, :] = _rms(o, gdn) * _silu(gate_ref[0, h:h + 1, :])


def _dn_step(x3, cst4, cw3, a3, b3, alog, dtb, gate3, gdn, state):
    nb = x3.shape[0]
    r3 = lambda s: pl.BlockSpec((1,) + s, lambda b: (b,) + (0,) * len(s))
    return pl.pallas_call(
        _dn_step_body,
        grid=(nb,),
        in_specs=[r3((3 * H_DN, HEAD_DIM)), r3((CONV_W - 1, 3 * H_DN, HEAD_DIM)), _whole(),
                  r3((H_DN, 1)), r3((H_DN, 1)), _whole(), _whole(), r3((H_DN, HEAD_DIM)), _whole(),
                  r3((H_DN, HEAD_DIM, HEAD_DIM))],
        out_specs=[r3((H_DN, HEAD_DIM)), r3((CONV_W - 1, 3 * H_DN, HEAD_DIM)), r3((H_DN, HEAD_DIM, HEAD_DIM))],
        out_shape=[jax.ShapeDtypeStruct((nb, H_DN, HEAD_DIM), F32),
                   jax.ShapeDtypeStruct((nb, CONV_W - 1, 3 * H_DN, HEAD_DIM), F32),
                   jax.ShapeDtypeStruct((nb, H_DN, HEAD_DIM, HEAD_DIM), F32)],
        compiler_params=_cparams(("arbitrary",)),
        name="deltanet_step",
    )(x3, cst4, cw3, a3, b3, alog, dtb, gate3, gdn, state)


PAGES_PER_BLOCK = MOBA_BLOCK // PAGE_SIZE
KP_PAGES = 8


def _head_rows(qrow):
    r = lax.broadcasted_iota(I32, (H_ATT, D_ATT), 0)
    c = lax.broadcasted_iota(I32, (H_ATT, D_ATT), 1)
    return jnp.where(lax.shift_right_logical(c, HEAD_DIM.bit_length() - 1) == r,
                     jnp.broadcast_to(qrow, (H_ATT, D_ATT)), 0.0)


def _kpass_body(pt_ref, q_ref, *refs, n_blocks):
    kpages = refs[:KP_PAGES]
    lg_ref, idx_ref, ssum_sc = refs[KP_PAGES:]
    g = pl.program_id(1)
    ng = pl.num_programs(1)

    @pl.when(g == 0)
    def _():
        ssum_sc[...] = jnp.zeros_like(ssum_sc)

    qh, ql = _split(_head_rows(q_ref[0]))
    lane = lax.broadcasted_iota(I32, (H_ATT, LANES), 1)
    ssum = ssum_sc[...]
    for j in range(KP_PAGES):
        kp = kpages[j][0].reshape(D_ATT, PAGE_SIZE)
        kh, kl = _split(kp)
        lg = _dot(qh, kh) + (_dot(qh, kl) + _dot(ql, kh))
        lg_ref[0, :, j * PAGE_SIZE:(j + 1) * PAGE_SIZE] = lg
        blk = (g * KP_PAGES + j) // PAGES_PER_BLOCK
        ssum = jnp.where(lane == blk, ssum + jnp.sum(lg, -1, keepdims=True), ssum)
    ssum_sc[...] = ssum

    @pl.when(g == ng - 1)
    def _():
        cur = jnp.where(lane < n_blocks, ssum * (1.0 / MOBA_BLOCK), NEG)
        out = jnp.zeros((H_ATT, LANES), I32)
        for j in range(MOBA_TOPK):
            mx = jnp.max(cur, -1, keepdims=True)
            first = jnp.min(jnp.where(cur == mx, lane, LANES), -1, keepdims=True)
            out = jnp.where(lane == j, first, out)
            cur = jnp.where(lane == first, NEG, cur)
        idx_ref[0] = out


def _kpass(page_table, q3, cache_kt):
    nb, n_pages = page_table.shape
    n_blocks = n_pages // PAGES_PER_BLOCK
    assert n_blocks <= LANES and n_pages % KP_PAGES == 0 and KP_PAGES % PAGES_PER_BLOCK == 0
    page_spec = lambda j: pl.BlockSpec((1, H_ATT, HEAD_DIM, PAGE_SIZE),
                                       lambda b, g, pt: (pt[b, g * KP_PAGES + j], 0, 0, 0))
    grid_spec = pltpu.PrefetchScalarGridSpec(
        num_scalar_prefetch=1,
        grid=(nb, n_pages // KP_PAGES),
        in_specs=[pl.BlockSpec((1, 1, D_ATT), lambda b, g, pt: (b, 0, 0))]
        + [page_spec(j) for j in range(KP_PAGES)],
        out_specs=[pl.BlockSpec((1, H_ATT, KP_PAGES * PAGE_SIZE), lambda b, g, pt: (b, 0, g)),
                   pl.BlockSpec((1, H_ATT, LANES), lambda b, g, pt: (b, 0, 0))],
        scratch_shapes=[pltpu.VMEM((H_ATT, LANES), F32)],
    )
    return pl.pallas_call(
        functools.partial(_kpass_body, n_blocks=n_blocks),
        grid_spec=grid_spec,
        out_shape=[jax.ShapeDtypeStruct((nb, H_ATT, n_pages * PAGE_SIZE), F32),
                   jax.ShapeDtypeStruct((nb, H_ATT, LANES), I32)],
        compiler_params=_cparams(("arbitrary", "arbitrary")),
        name="moba_decode_kpass",
    )(page_table, q3, *([cache_kt] * KP_PAGES))


VP_PAGES = H_ATT * MOBA_TOPK * PAGES_PER_BLOCK


def _vpass_body(pt_ref, idx_ref, lg_ref, q_ref, kn_ref, vn_ref, bkt_ref, tblt_ref, cv_ref, o_ref, vbuf, sem,
                *, n_blocks):
    b = pl.program_id(0)

    def slab_copy(h, j, pg):
        n = (h * MOBA_TOPK + j) * PAGES_PER_BLOCK + pg
        blk = jnp.clip(idx_ref[b, h, j], 0, n_blocks - 1)
        page = pt_ref[b, blk * PAGES_PER_BLOCK + pg]
        return pltpu.make_async_copy(cv_ref.at[page, h], vbuf.at[n], sem.at[n])

    slabs = [(h, j, pg) for h in range(H_ATT) for j in range(MOBA_TOPK) for pg in range(PAGES_PER_BLOCK)]
    for s in slabs:
        slab_copy(*s).start()
    scale = HEAD_DIM ** -0.5
    bkt = bkt_ref[...]
    tblt = tblt_ref[...]
    bias_last = jnp.zeros((H_ATT, MOBA_BLOCK), F32)
    for bb in range(N_BUCKETS):
        bias_last = jnp.where(bkt == bb, tblt[:, bb:bb + 1], bias_last)
    bias_far = tblt[:, N_BUCKETS - 1:N_BUCKETS]
    lg_self = jnp.sum(q_ref[0] * kn_ref[0], -1, keepdims=True) * scale + tblt[:, 0:1]

    ps, dens, accs = [], [], []
    for h in range(H_ATT):
        xs = []
        ls = lg_self[h:h + 1]
        mx = ls
        for j in range(MOBA_TOPK):
            blk = jnp.clip(idx_ref[b, h, j], 0, n_blocks - 1)
            start = pl.multiple_of(blk * MOBA_BLOCK, MOBA_BLOCK)
            raw = lg_ref[0, h:h + 1, pl.ds(start, MOBA_BLOCK)]
            x = raw * scale + jnp.where(blk == n_blocks - 1, bias_last[h:h + 1], bias_far[h:h + 1])
            xs.append(x)
            mx = jnp.maximum(mx, jnp.max(x, -1, keepdims=True))
        p_self = jnp.exp(ls - mx)
        den = p_self
        pjs = []
        for j in range(MOBA_TOPK):
            pj = jnp.exp(xs[j] - mx)
            den = den + jnp.sum(pj, -1, keepdims=True)
            pjs.append(pj.astype(BF16))
        ps.append(pjs)
        dens.append(den)
        accs.append(p_self * vn_ref[0, h:h + 1])

    for s in slabs:
        slab_copy(*s).wait()
    for h in range(H_ATT):
        acc = accs[h]
        for j in range(MOBA_TOPK):
            for pg in range(PAGES_PER_BLOCK):
                vt = vbuf[(h * MOBA_TOPK + j) * PAGES_PER_BLOCK + pg]
                acc = acc + _bdot(ps[h][j][:, pg * PAGE_SIZE:(pg + 1) * PAGE_SIZE], vt, NT)
        o_ref[0, h:h + 1, :] = acc / dens[h]


def _vpass(page_table, idx, logits, q3, kn3, vn3, bkt_last, tbl_t, cache_vt):
    nb, n_pages = page_table.shape
    n_blocks = n_pages // PAGES_PER_BLOCK
    r3 = lambda s: pl.BlockSpec((1,) + s, lambda b, pt, ix: (b,) + (0,) * len(s))
    grid_spec = pltpu.PrefetchScalarGridSpec(
        num_scalar_prefetch=2,
        grid=(nb,),
        in_specs=[r3((H_ATT, n_pages * PAGE_SIZE)), r3((H_ATT, HEAD_DIM)), r3((H_ATT, HEAD_DIM)),
                  r3((H_ATT, HEAD_DIM)), _whole(), _whole(), pl.BlockSpec(memory_space=pl.ANY)],
        out_specs=r3((H_ATT, HEAD_DIM)),
        scratch_shapes=[pltpu.VMEM((VP_PAGES, HEAD_DIM, PAGE_SIZE), F32), pltpu.SemaphoreType.DMA((VP_PAGES,))],
    )
    return pl.pallas_call(
        functools.partial(_vpass_body, n_blocks=n_blocks),
        grid_spec=grid_spec,
        out_shape=jax.ShapeDtypeStruct((nb, H_ATT, HEAD_DIM), F32),
        compiler_params=_cparams(("arbitrary",)),
        name="moba_decode_vpass",
    )(page_table, idx, logits, q3, kn3, vn3, bkt_last, tbl_t, cache_vt)


def _pair_layout_ab(w_a, w_b):
    d = w_a.shape[0]
    out = jnp.zeros((d, N_PAIR, LANES), w_a.dtype)
    out = out.at[:, :, 0:2].set(w_a.reshape(d, N_PAIR, 2))
    out = out.at[:, :, 2:4].set(w_b.reshape(d, N_PAIR, 2))
    return out.reshape(d, N_PAIR * LANES)


def kernel(x_prompt, x_sample, p_prompt, p_sample, cache_k, cache_v, state_dn, state_conv, page_table, w_in, conv_w, a_log, dt_bias, g_dn, w_o, rel_bias, g_pre_mix, g_post_mix, g_pre_mlp, g_post_mlp, w_up, w_down, g_ple, w_pg, w_ple):
    batch, seq, d = x_prompt.shape
    nb = x_sample.shape[0]
    depth = w_in.shape[0]
    assert depth == 1 and x_sample.shape[1] == 1
    assert seq % MOBA_KV == 0 and seq // MOBA_BLOCK <= LANES - HEAD_DIM
    assert page_table.shape[1] * PAGE_SIZE // MOBA_BLOCK >= MOBA_TOPK
    l = 0
    row = lambda a: a.reshape(1, -1)

    wi = w_in[l]
    o_gate = D_CONV + D_DN
    o_att = o_gate + 2 * H_DN
    w_main = jnp.concatenate([wi[:, :o_gate], wi[:, o_att:]], axis=1).astype(BF16)
    w_ab = _pair_layout_ab(wi[:, o_gate:o_gate + H_DN], wi[:, o_gate + H_DN:o_att]).astype(BF16)
    hpar = jnp.zeros((N_PAIR, 8, LANES), F32)
    hpar = hpar.at[:, 0, 0:2].set(a_log[l].reshape(N_PAIR, 2)).at[:, 1, 0:2].set(dt_bias[l].reshape(N_PAIR, 2))
    gdn2 = jnp.tile(g_dn[l], 2).reshape(1, LANES)
    wo, wup, wdn, wpg, wple = (w.astype(BF16) for w in (w_o[l], w_up[l], w_down[l], w_pg[l], w_ple[l]))
    tail_w = (wo, row(g_post_mix[l]), row(g_pre_mlp[l]), wup, wdn, row(g_post_mlp[l]), row(g_ple[l]), wpg, wple)

    m = batch * seq
    xp = x_prompt.reshape(m, d)
    w_kvt = wi[:, o_att + D_ATT:].T.astype(BF16)
    qkv_raw, gate, ab, q_att, k_att, k_t, v_t, kmeans = _inproj(xp, row(g_pre_mix[l]), w_main, w_ab, 512,
                                                                 w_kvt=w_kvt, seq=seq)
    y_dn, sfin = _dn_prompt(qkv_raw, gate, ab, conv_w[l], hpar, gdn2, batch)
    nblk = seq // MOBA_BLOCK
    km_pad = jnp.zeros((batch, LANES, D_ATT), F32).at[:, :nblk].set(kmeans.reshape(batch, nblk, D_ATT))
    bsd = lambda a: a.reshape(batch, seq, D_ATT)
    o_at = _moba_prompt(bsd(q_att), bsd(k_att), v_t, km_pad, rel_bias).reshape(m, D_ATT)
    y_prompt = _tail(xp, y_dn, o_at, p_prompt[l].reshape(m, -1), *tail_w, tile=256).reshape(batch, seq, d)

    to_rows = lambda a: jnp.transpose(a.reshape(1, batch, H_ATT, HEAD_DIM, seq), (0, 1, 4, 2, 3))
    k_prompt = to_rows(k_t)
    v_prompt = to_rows(v_t)
    sf = sfin.reshape(batch, N_PAIR, 2, HEAD_DIM, 2, HEAD_DIM)
    dn_prompt = jnp.stack([sf[:, :, 0, :, 0, :], sf[:, :, 1, :, 1, :]], axis=2).reshape(1, batch, H_DN, HEAD_DIM, HEAD_DIM)
    conv_prompt = qkv_raw.reshape(batch, seq, D_CONV)[:, seq - (CONV_W - 1):, :].reshape(1, batch, CONV_W - 1, D_CONV)

    xs = x_sample.reshape(nb, d)
    qkv_s, gate_s, ab_s, q_s, k_s, v_s = _inproj(xs, row(g_pre_mix[l]), w_main, w_ab, nb)
    ab4 = ab_s.reshape(nb, N_PAIR, LANES)
    a3 = ab4[:, :, 0:2].reshape(nb, H_DN, 1)
    b3 = ab4[:, :, 2:4].reshape(nb, H_DN, 1)
    y3, cnew, snew = _dn_step(
        qkv_s.reshape(nb, 3 * H_DN, HEAD_DIM), state_conv[l].reshape(nb, CONV_W - 1, 3 * H_DN, HEAD_DIM),
        conv_w[l].reshape(CONV_W, 3 * H_DN, HEAD_DIM), a3, b3, a_log[l].reshape(H_DN, 1), dt_bias[l].reshape(H_DN, 1),
        gate_s.reshape(nb, H_DN, HEAD_DIM), row(g_dn[l]), state_dn[l])
    n_pages = page_table.shape[1]
    n_pool = cache_k.shape[1]
    ckt = jnp.transpose(cache_k[l], (0, 2, 3, 1))
    cvt = jnp.transpose(cache_v[l], (0, 2, 3, 1))
    logits, idx = _kpass(page_table, q_s.reshape(nb, 1, D_ATT), ckt)
    hd = lambda a: a.reshape(nb, H_ATT, HEAD_DIM)
    bkt_last = jnp.asarray(_t5_bucket_np(MOBA_BLOCK - np.arange(MOBA_BLOCK)).reshape(1, MOBA_BLOCK))
    o_s = _vpass(page_table, idx[:, :, :MOBA_TOPK], logits, hd(q_s), hd(k_s), hd(v_s), bkt_last, rel_bias.T, cvt)
    y_sample = _tail(xs, y3.reshape(nb, D_DN), o_s.reshape(nb, D_ATT), p_sample[l].reshape(nb, -1), *tail_w,
                     tile=nb).reshape(nb, 1, d)

    k_sample = k_s.reshape(1, nb, 1, H_ATT, HEAD_DIM)
    v_sample = v_s.reshape(1, nb, 1, H_ATT, HEAD_DIM)
    dn_sample = snew.reshape(1, nb, H_DN, HEAD_DIM, HEAD_DIM)
    conv_sample = cnew.reshape(1, nb, CONV_W - 1, D_CONV)
    return (y_prompt, y_sample, k_prompt, v_prompt, dn_prompt, conv_prompt,
            k_sample, v_sample, dn_sample, conv_sample)
```
